```python
import jax, jax.numpy as jnp
from jax import lax
import numpy as np

D_MODEL = 1024
BATCH = 16
SEQ = 4096
DEPTH = 2

N_MIXERS = 2
N_MEM = 256
MIX_WIDTH = D_MODEL
MEM_HEADS = 4
MEM_WIDTH = MIX_WIDTH // 4
MEM_HEAD_DIM = MEM_WIDTH // MEM_HEADS
MAIN_WIDTH = MIX_WIDTH - MEM_WIDTH
MLA_HEADS = 12
QK_NOPE = 64
QK_ROPE = 32
V_HEAD = MAIN_WIDTH // MLA_HEADS
Q_LORA = (3 * D_MODEL) // 8
KV_LORA = D_MODEL // 4
ROPE_THETA = 10000.0
Q_BLOCK = 128
FNET_GROUPS = 4
FNET_GROUP_DIM = MAIN_WIDTH // FNET_GROUPS
EPS = 1e-6

MLA_IN = Q_LORA + KV_LORA + QK_ROPE + MEM_WIDTH + MIX_WIDTH
FNET_IN = MAIN_WIDTH + MEM_WIDTH + MIX_WIDTH

kernel_name = "hybrid_mla_fnet_memory_encoder"


def rmsnorm(x, g):
    xf = x.astype(jnp.float32)
    y = xf * lax.rsqrt(jnp.mean(xf * xf, axis=-1, keepdims=True) + EPS)
    return (y * g.astype(jnp.float32)).astype(x.dtype)


def rope_tables(positions):
    inv_freq = 1.0 / (ROPE_THETA ** (jnp.arange(0, QK_ROPE, 2, dtype=jnp.float32) / QK_ROPE))
    ang = positions.astype(jnp.float32)[..., None] * inv_freq
    return jnp.cos(ang), jnp.sin(ang)


def apply_rope(t, cos, sin):
    tf = t.astype(jnp.float32)
    t1, t2 = tf[..., : QK_ROPE // 2], tf[..., QK_ROPE // 2:]
    return jnp.concatenate([t1 * cos - t2 * sin, t2 * cos + t1 * sin], axis=-1).astype(t.dtype)


def mla_attention(q_nope, q_rope, k_nope, k_rope, v):
    B, S, H, _ = q_nope.shape
    nb = S // Q_BLOCK
    scale = 1.0 / float(np.sqrt(QK_NOPE + QK_ROPE))

    def to_blocks(t):
        return jnp.moveaxis(t.reshape((B, nb, Q_BLOCK) + t.shape[2:]), 1, 0)

    def one_block(args):
        qn, qr = args
        s = (jnp.einsum('bqhd,bkhd->bhqk', qn, k_nope)
             + jnp.einsum('bqhr,bkr->bhqk', qr, k_rope)).astype(jnp.float32) * scale
        p = jax.nn.softmax(s, axis=-1).astype(v.dtype)
        return jnp.einsum('bhqk,bkhd->bqhd', p, v)

    out = lax.map(one_block, (to_blocks(q_nope), to_blocks(q_rope)))
    return jnp.moveaxis(out, 0, 1).reshape(B, S, H * V_HEAD)


def memory_cross_attention(q_mem, mem, mem_norm_g, w_mem_kv):
    B, S, _ = q_mem.shape
    q = q_mem.reshape(B, S, MEM_HEADS, MEM_HEAD_DIM)
    kv = (rmsnorm(mem, mem_norm_g) @ w_mem_kv).reshape(B, mem.shape[1], 2, MEM_HEADS, MEM_HEAD_DIM)
    k, v = kv[:, :, 0], kv[:, :, 1]
    s = jnp.einsum('bshd,bmhd->bhsm', q, k).astype(jnp.float32) * (1.0 / float(np.sqrt(MEM_HEAD_DIM)))
    p = jax.nn.softmax(s, axis=-1).astype(v.dtype)
    return jnp.einsum('bhsm,bmhd->bshd', p, v).reshape(B, S, MEM_WIDTH)


def mla_layer(x, mem, cos, sin, norm_g, w_in, q_norm_g, kv_norm_g, w_uq, w_ukv,
              mem_norm_g, w_mem_kv, w_out):
    B, S, _ = x.shape
    h = rmsnorm(x, norm_g)
    proj = h @ w_in
    o1 = Q_LORA
    o2 = o1 + KV_LORA
    o3 = o2 + QK_ROPE
    o4 = o3 + MEM_WIDTH
    c_q, c_kv, k_rope = proj[..., :o1], proj[..., o1:o2], proj[..., o2:o3]
    q_mem, gate = proj[..., o3:o4], proj[..., o4:]
    q = (rmsnorm(c_q, q_norm_g) @ w_uq).reshape(B, S, MLA_HEADS, QK_NOPE + QK_ROPE)
    kv = (rmsnorm(c_kv, kv_norm_g) @ w_ukv).reshape(B, S, MLA_HEADS, QK_NOPE + V_HEAD)
    q_nope = q[..., :QK_NOPE]
    q_rope = apply_rope(q[..., QK_NOPE:], cos[:, :, None, :], sin[:, :, None, :])
    k_nope, v = kv[..., :QK_NOPE], kv[..., QK_NOPE:]
    k_rope = apply_rope(k_rope, cos, sin)
    attn = mla_attention(q_nope, q_rope, k_nope, k_rope, v)
    mem_out = memory_cross_attention(q_mem, mem, mem_norm_g, w_mem_kv)
    branch = jnp.concatenate([attn, mem_out], axis=-1) * jax.nn.silu(gate)
    return x + branch @ w_out


def fnet_layer(x, mem, norm_g, w_in, w_fnet, mem_norm_g, w_mem_kv, w_out):
    B, S, _ = x.shape
    h = rmsnorm(x, norm_g)
    proj = h @ w_in
    f = proj[..., :MAIN_WIDTH].reshape(B, S, FNET_GROUPS, FNET_GROUP_DIM).astype(jnp.float32)
    q_mem = proj[..., MAIN_WIDTH:MAIN_WIDTH + MEM_WIDTH]
    gate = proj[..., MAIN_WIDTH + MEM_WIDTH:]
    spec = jnp.fft.fft2(f, axes=(1, 3), norm='ortho').real
    mixed = jnp.einsum('bsgc,gcd->bsgd', spec, w_fnet.astype(jnp.float32))
    mixed = mixed.reshape(B, S, MAIN_WIDTH).astype(x.dtype)
    mem_out = memory_cross_attention(q_mem, mem, mem_norm_g, w_mem_kv)
    branch = jnp.concatenate([mixed, mem_out], axis=-1) * jax.nn.silu(gate)
    return x + branch @ w_out


def setup_inputs(seed: int = 0) -> dict:
    key = jax.random.key(seed)
    ks = jax.random.split(key, 24)
    f32 = jnp.float32

    def w(k, shape, fan_in):
        return jax.random.normal(k, shape, f32) * (fan_in ** -0.5)

    def gain(k, n):
        return 1.0 + 0.02 * jax.random.normal(k, (n,), f32)

    x = jax.random.normal(ks[0], (BATCH, SEQ, D_MODEL), f32)
    mem = jax.random.normal(ks[1], (BATCH, N_MEM, D_MODEL), f32)
    offsets = jax.random.randint(ks[2], (BATCH, 1), 0, 1024, dtype=jnp.int32)
    positions = (jnp.arange(SEQ, dtype=jnp.int32)[None, :] + offsets).astype(jnp.int32)
    return {
        "x": x,
        "mem": mem,
        "positions": positions,
        "norm_g_l0": gain(ks[3], D_MODEL),
        "w_in_l0": w(ks[4], (D_MODEL, MLA_IN), D_MODEL),
        "q_norm_g_l0": gain(ks[5], Q_LORA),
        "kv_norm_g_l0": gain(ks[6], KV_LORA),
        "w_uq_l0": w(ks[7], (Q_LORA, MLA_HEADS * (QK_NOPE + QK_ROPE)), Q_LORA),
        "w_ukv_l0": w(ks[8], (KV_LORA, MLA_HEADS * (QK_NOPE + V_HEAD)), KV_LORA),
        "mem_norm_g_l0": gain(ks[9], D_MODEL),
        "w_mem_kv_l0": w(ks[10], (D_MODEL, 2 * MEM_WIDTH), D_MODEL),
        "w_out_l0": w(ks[11], (MIX_WIDTH, D_MODEL), MIX_WIDTH),
        "norm_g_l1": gain(ks[12], D_MODEL),
        "w_in_l1": w(ks[13], (D_MODEL, FNET_IN), D_MODEL),
        "w_fnet_l1": w(ks[14], (FNET_GROUPS, FNET_GROUP_DIM, FNET_GROUP_DIM), FNET_GROUP_DIM),
        "mem_norm_g_l1": gain(ks[15], D_MODEL),
        "w_mem_kv_l1": w(ks[16], (D_MODEL, 2 * MEM_WIDTH), D_MODEL),
        "w_out_l1": w(ks[17], (MIX_WIDTH, D_MODEL), MIX_WIDTH),
        "final_norm_g": gain(ks[18], D_MODEL),
    }


def reference(x, mem, positions,
              norm_g_l0, w_in_l0, q_norm_g_l0, kv_norm_g_l0, w_uq_l0, w_ukv_l0,
              mem_norm_g_l0, w_mem_kv_l0, w_out_l0,
              norm_g_l1, w_in_l1, w_fnet_l1, mem_norm_g_l1, w_mem_kv_l1, w_out_l1,
              final_norm_g):
    cos, sin = rope_tables(positions)
    for i in range(DEPTH):
        if i % N_MIXERS == 0:
            x = mla_layer(x, mem, cos, sin, norm_g_l0, w_in_l0, q_norm_g_l0, kv_norm_g_l0,
                          w_uq_l0, w_ukv_l0, mem_norm_g_l0, w_mem_kv_l0, w_out_l0)
        else:
            x = fnet_layer(x, mem, norm_g_l1, w_in_l1, w_fnet_l1,
                           mem_norm_g_l1, w_mem_kv_l1, w_out_l1)
    return rmsnorm(x, final_norm_g)
```

```python
import functools

import numpy as np
import jax
import jax.numpy as jnp
from jax import lax
from jax.experimental import pallas as pl
from jax.experimental.pallas import tpu as pltpu

F32 = jnp.float32
BF16 = jnp.bfloat16

D_MODEL = 1024
N_MEM = 256
MIX_WIDTH = D_MODEL
MEM_HEADS = 4
MEM_WIDTH = MIX_WIDTH // 4
MEM_HEAD_DIM = MEM_WIDTH // MEM_HEADS
MAIN_WIDTH = MIX_WIDTH - MEM_WIDTH
MLA_HEADS = 12
QK_NOPE = 64
QK_ROPE = 32
V_HEAD = MAIN_WIDTH // MLA_HEADS
Q_LORA = (3 * D_MODEL) // 8
KV_LORA = D_MODEL // 4
ROPE_THETA = 10000.0
FNET_GROUPS = 4
FNET_GROUP_DIM = MAIN_WIDTH // FNET_GROUPS
EPS = 1e-6

LANES = 128
HEAD_PAD = LANES
QK_PAD = MLA_HEADS * HEAD_PAD
ROPE_LO = QK_NOPE
ROPE_HALF = QK_ROPE // 2
GROUP_PAD = 256

FFT_R = 64

TM = 512
TQ = 512
TKV = 512
FFT_N2_PER_STEP = 8

VMEM_LIMIT = 56 * 1024 * 1024


def _rmsnorm(x, g):
    return x * lax.rsqrt(jnp.mean(x * x, axis=-1, keepdims=True) + EPS) * g


def _params(n_grid_dims):
    return pltpu.CompilerParams(
        dimension_semantics=("arbitrary",) * n_grid_dims,
        vmem_limit_bytes=VMEM_LIMIT,
    )


def _rope_table_kernel(pos_ref, invf_ref, cos_ref, sin_ref):
    ang = invf_ref[...] * pos_ref[0].astype(F32)
    row = lax.broadcasted_iota(jnp.int32, ang.shape, 0)
    cos_ref[0] = jnp.cos(ang)
    sin_ref[0] = jnp.where(row < ROPE_HALF, -jnp.sin(ang), jnp.sin(ang))


def _rope_tables(positions):
    b, s = positions.shape
    inv_freq = 1.0 / (ROPE_THETA ** (jnp.arange(0, QK_ROPE, 2, dtype=F32) / QK_ROPE))
    invf = jnp.concatenate([inv_freq, inv_freq]).reshape(QK_ROPE, 1)
    cos_t, sin_t = pl.pallas_call(
        _rope_table_kernel,
        grid=(b,),
        in_specs=[pl.BlockSpec((1, 1, s), lambda i: (i, 0, 0)),
                  pl.BlockSpec((QK_ROPE, 1), lambda i: (0, 0))],
        out_specs=[pl.BlockSpec((1, QK_ROPE, s), lambda i: (i, 0, 0))] * 2,
        out_shape=[jax.ShapeDtypeStruct((b, QK_ROPE, s), F32)] * 2,
        compiler_params=_params(1),
        name="rope_tables",
    )(positions.reshape(b, 1, s), invf)
    pad = ((0, 0), (0, 0), (ROPE_LO, HEAD_PAD - ROPE_LO - QK_ROPE))
    cos_a = jnp.pad(jnp.swapaxes(cos_t, 1, 2), pad, constant_values=1.0)
    sin_b = jnp.pad(jnp.swapaxes(sin_t, 1, 2), pad, constant_values=0.0)
    return cos_a, sin_b


def _mem_kv_kernel(mem_ref, g_ref, w_ref, kbd_ref, vbd_ref):
    mn = _rmsnorm(mem_ref[0], g_ref[...]).astype(BF16)
    kv = jnp.dot(mn, w_ref[...], preferred_element_type=F32)
    k_t = (kv[:, :MEM_WIDTH] * (1.0 / float(np.sqrt(MEM_HEAD_DIM)))).T
    v = kv[:, MEM_WIDTH:]
    k_rep = jnp.concatenate([k_t] * MEM_HEADS, axis=1)
    r = lax.broadcasted_iota(jnp.int32, k_rep.shape, 0) // MEM_HEAD_DIM
    c = lax.broadcasted_iota(jnp.int32, k_rep.shape, 1) // N_MEM
    kbd_ref[0] = jnp.where(r == c, k_rep, 0.0).astype(BF16)
    v_rep = jnp.concatenate([v] * MEM_HEADS, axis=0)
    r = lax.broadcasted_iota(jnp.int32, v_rep.shape, 0) // N_MEM
    c = lax.broadcasted_iota(jnp.int32, v_rep.shape, 1) // MEM_HEAD_DIM
    vbd_ref[0] = jnp.where(r == c, v_rep, 0.0).astype(BF16)


def _mem_kv(mem, g, w_mem_kv):
    b = mem.shape[0]
    return pl.pallas_call(
        _mem_kv_kernel,
        grid=(b,),
        in_specs=[pl.BlockSpec((1, N_MEM, D_MODEL), lambda i: (i, 0, 0)),
                  pl.BlockSpec((1, D_MODEL), lambda i: (0, 0)),
                  pl.BlockSpec((D_MODEL, 2 * MEM_WIDTH), lambda i: (0, 0))],
        out_specs=[pl.BlockSpec((1, MEM_WIDTH, MEM_HEADS * N_MEM), lambda i: (i, 0, 0)),
                   pl.BlockSpec((1, MEM_HEADS * N_MEM, MEM_WIDTH), lambda i: (i, 0, 0))],
        out_shape=[jax.ShapeDtypeStruct((b, MEM_WIDTH, MEM_HEADS * N_MEM), BF16),
                   jax.ShapeDtypeStruct((b, MEM_HEADS * N_MEM, MEM_WIDTH), BF16)],
        compiler_params=_params(1),
        name="mem_kv",
    )(mem, g.reshape(1, D_MODEL), w_mem_kv.astype(BF16))


_L0_CQ = (0, Q_LORA)
_L0_CKV = (_L0_CQ[1], _L0_CQ[1] + KV_LORA)
_L0_QM = (_L0_CKV[1], _L0_CKV[1] + MEM_WIDTH)
_L0_GATE = (_L0_QM[1], _L0_QM[1] + MIX_WIDTH)
_L0_KR = (_L0_GATE[1], _L0_GATE[1] + HEAD_PAD)
_L0_IN = _L0_KR[1]


def _l0_pre_kernel(x_ref, cos_ref, sin_ref, g_ref, w_in_ref, gq_ref, gkv_ref, wq_ref, wk_ref,
                   wvt_ref, q_ref, k_ref, vt_ref, qm_ref, gate_ref):
    h = _rmsnorm(x_ref[0], g_ref[...]).astype(BF16)
    proj = jnp.dot(h, w_in_ref[...], preferred_element_type=F32)
    qm_ref[0] = proj[:, _L0_QM[0]:_L0_QM[1]].astype(BF16)
    gate_ref[0] = proj[:, _L0_GATE[0]:_L0_GATE[1]].astype(BF16)
    cqn = _rmsnorm(proj[:, _L0_CQ[0]:_L0_CQ[1]], gq_ref[...]).astype(BF16)
    ckvn = _rmsnorm(proj[:, _L0_CKV[0]:_L0_CKV[1]], gkv_ref[...]).astype(BF16)
    q = jnp.dot(cqn, wq_ref[...], preferred_element_type=F32)
    kn = jnp.dot(ckvn, wk_ref[...], preferred_element_type=F32)
    vt = lax.dot_general(wvt_ref[...], ckvn, (((1,), (1,)), ((), ())),
                         preferred_element_type=F32)
    vt_ref[0] = vt.astype(BF16)

    cos_a = cos_ref[0]
    sin_b = sin_ref[0]
    lane = lax.broadcasted_iota(jnp.int32, cos_a.shape, 1)
    first_half = lane < ROPE_LO + ROPE_HALF

    def rope(t):
        partner = jnp.where(first_half, pltpu.roll(t, HEAD_PAD - ROPE_HALF, 1),
                            pltpu.roll(t, ROPE_HALF, 1))
        return t * cos_a + partner * sin_b

    k_rope = rope(proj[:, _L0_KR[0]:_L0_KR[1]])
    scale = 1.0 / float(np.sqrt(QK_NOPE + QK_ROPE))
    for hd in range(MLA_HEADS):
        sl = slice(hd * HEAD_PAD, (hd + 1) * HEAD_PAD)
        q_ref[0, :, sl] = (rope(q[:, sl]) * scale).astype(BF16)
        k_ref[0, :, sl] = (kn[:, sl] + k_rope).astype(BF16)


def _l0_pre(x, cos_a, sin_b, norm_g, w_in, q_norm_g, kv_norm_g, w_uq, w_ukv):
    b, s, _ = x.shape
    o1, o2, o3, o4 = Q_LORA, Q_LORA + KV_LORA, Q_LORA + KV_LORA + QK_ROPE, Q_LORA + KV_LORA + QK_ROPE + MEM_WIDTH
    w_kr = jnp.pad(w_in[:, o2:o3], ((0, 0), (ROPE_LO, HEAD_PAD - ROPE_LO - QK_ROPE)))
    w_in_p = jnp.concatenate([w_in[:, :o2], w_in[:, o3:o4], w_in[:, o4:], w_kr], axis=1).astype(BF16)
    wq = jnp.pad(w_uq.reshape(Q_LORA, MLA_HEADS, QK_NOPE + QK_ROPE),
                 ((0, 0), (0, 0), (0, HEAD_PAD - QK_NOPE - QK_ROPE))).reshape(Q_LORA, QK_PAD).astype(BF16)
    w_ukv3 = w_ukv.reshape(KV_LORA, MLA_HEADS, QK_NOPE + V_HEAD)
    wk = jnp.pad(w_ukv3[:, :, :QK_NOPE], ((0, 0), (0, 0), (0, HEAD_PAD - QK_NOPE))
                 ).reshape(KV_LORA, QK_PAD).astype(BF16)
    wvt = w_ukv3[:, :, QK_NOPE:].reshape(KV_LORA, MAIN_WIDTH).T.astype(BF16)

    row = lambda i, j: (i, j, 0)
    const = lambda i, j: (0, 0)
    return pl.pallas_call(
        _l0_pre_kernel,
        grid=(b, s // TM),
        in_specs=[pl.BlockSpec((1, TM, D_MODEL), row),
                  pl.BlockSpec((1, TM, HEAD_PAD), row),
                  pl.BlockSpec((1, TM, HEAD_PAD), row),
                  pl.BlockSpec((1, D_MODEL), const),
                  pl.BlockSpec((D_MODEL, _L0_IN), const),
                  pl.BlockSpec((1, Q_LORA), const),
                  pl.BlockSpec((1, KV_LORA), const),
                  pl.BlockSpec((Q_LORA, QK_PAD), const),
                  pl.BlockSpec((KV_LORA, QK_PAD), const),
                  pl.BlockSpec((MAIN_WIDTH, KV_LORA), const)],
        out_specs=[pl.BlockSpec((1, TM, QK_PAD), row),
                   pl.BlockSpec((1, TM, QK_PAD), row),
                   pl.BlockSpec((1, MAIN_WIDTH, TM), lambda i, j: (i, 0, j)),
                   pl.BlockSpec((1, TM, MEM_WIDTH), row),
                   pl.BlockSpec((1, TM, MIX_WIDTH), row)],
        out_shape=[jax.ShapeDtypeStruct((b, s, QK_PAD), BF16),
                   jax.ShapeDtypeStruct((b, s, QK_PAD), BF16),
                   jax.ShapeDtypeStruct((b, MAIN_WIDTH, s), BF16),
                   jax.ShapeDtypeStruct((b, s, MEM_WIDTH), BF16),
                   jax.ShapeDtypeStruct((b, s, MIX_WIDTH), BF16)],
        compiler_params=_params(2),
        name="l0_pre",
    )(x, cos_a, sin_b, norm_g.reshape(1, -1), w_in_p, q_norm_g.reshape(1, -1),
      kv_norm_g.reshape(1, -1), wq, wk, wvt)


HEADS_PER_STEP = 2


def _attn_kernel(q_ref, k_ref, vt_ref, o_ref, m_ref, l_ref, acc_ref):
    s_len = k_ref.shape[1]
    m_ref[...] = jnp.full(m_ref.shape, -jnp.inf, F32)
    l_ref[...] = jnp.zeros(l_ref.shape, F32)
    acc_ref[...] = jnp.zeros(acc_ref.shape, F32)

    def body(c, carry):
        off = pl.multiple_of(c * TKV, TKV)
        for hd in range(HEADS_PER_STEP):
            lanes = slice(hd * HEAD_PAD, (hd + 1) * HEAD_PAD)
            rows = slice(hd * V_HEAD, (hd + 1) * V_HEAD)
            s_t = lax.dot_general(k_ref[0, pl.ds(off, TKV), lanes], q_ref[0, :, lanes],
                                  (((1,), (1,)), ((), ())), preferred_element_type=F32)
            m_old = m_ref[hd]
            m_new = jnp.maximum(m_old, jnp.max(s_t, axis=0, keepdims=True))
            alpha = jnp.exp(m_old - m_new)
            p = jnp.exp(s_t - m_new)
            l_ref[hd] = alpha * l_ref[hd] + jnp.sum(p, axis=0, keepdims=True)
            m_ref[hd] = m_new
            pv = jnp.dot(vt_ref[0, rows, pl.ds(off, TKV)], p.astype(BF16),
                         preferred_element_type=F32)
            acc_ref[rows, :] = alpha * acc_ref[rows, :] + pv
        return carry

    lax.fori_loop(0, s_len // TKV, body, 0)
    inv = jnp.concatenate(
        [jnp.broadcast_to(1.0 / l_ref[hd], (V_HEAD, l_ref.shape[2])) for hd in range(HEADS_PER_STEP)], axis=0)
    o_ref[0] = (acc_ref[...] * inv).T.astype(BF16)


def _attention(q, k, vt):
    b, s, _ = q.shape
    width = HEADS_PER_STEP * HEAD_PAD
    vrows = HEADS_PER_STEP * V_HEAD
    return pl.pallas_call(
        _attn_kernel,
        grid=(b, MLA_HEADS // HEADS_PER_STEP, s // TQ),
        in_specs=[pl.BlockSpec((1, TQ, width), lambda i, j, t: (i, t, j)),
                  pl.BlockSpec((1, s, width), lambda i, j, t: (i, 0, j)),
                  pl.BlockSpec((1, vrows, s), lambda i, j, t: (i, j, 0))],
        out_specs=pl.BlockSpec((1, TQ, vrows), lambda i, j, t: (i, t, j)),
        out_shape=jax.ShapeDtypeStruct((b, s, MAIN_WIDTH), BF16),
        scratch_shapes=[pltpu.VMEM((HEADS_PER_STEP, 1, TQ), F32),
                        pltpu.VMEM((HEADS_PER_STEP, 1, TQ), F32),
                        pltpu.VMEM((vrows, TQ), F32)],
        compiler_params=_params(3),
        name="mla_attention",
    )(q, k, vt)


def _post_kernel(*refs, final_norm):
    if final_norm:
        x_ref, main_ref, qm_ref, gate_ref, kbd_ref, vbd_ref, wout_ref, fg_ref, o_ref = refs
    else:
        x_ref, main_ref, qm_ref, gate_ref, kbd_ref, vbd_ref, wout_ref, o_ref = refs
    s = jnp.dot(qm_ref[0], kbd_ref[0], preferred_element_type=F32)
    probs = []
    for hd in range(MEM_HEADS):
        sh = s[:, hd * N_MEM:(hd + 1) * N_MEM]
        e = jnp.exp(sh - jnp.max(sh, axis=-1, keepdims=True))
        probs.append((e / jnp.sum(e, axis=-1, keepdims=True)).astype(BF16))
    mem_out = jnp.dot(jnp.concatenate(probs, axis=-1), vbd_ref[0], preferred_element_type=F32)
    gate = gate_ref[0].astype(F32)
    branch = jnp.concatenate([main_ref[0].astype(F32), mem_out], axis=-1) * (gate * jax.nn.sigmoid(gate))
    y = x_ref[0] + jnp.dot(branch.astype(BF16), wout_ref[...], preferred_element_type=F32)
    if final_norm:
        y = _rmsnorm(y, fg_ref[...])
    o_ref[0] = y


def _post(x, main, qm, gate, kbd, vbd, w_out, final_g=None):
    b, s, _ = x.shape
    row = lambda i, j: (i, j, 0)
    per_batch = lambda i, j: (i, 0, 0)
    const = lambda i, j: (0, 0)
    in_specs = [pl.BlockSpec((1, TM, D_MODEL), row),
                pl.BlockSpec((1, TM, MAIN_WIDTH), row),
                pl.BlockSpec((1, TM, MEM_WIDTH), row),
                pl.BlockSpec((1, TM, MIX_WIDTH), row),
                pl.BlockSpec((1, MEM_WIDTH, MEM_HEADS * N_MEM), per_batch),
                pl.BlockSpec((1, MEM_HEADS * N_MEM, MEM_WIDTH), per_batch),
                pl.BlockSpec((MIX_WIDTH, D_MODEL), const)]
    args = [x, main, qm, gate, kbd, vbd, w_out.astype(BF16)]
    if final_g is not None:
        in_specs.append(pl.BlockSpec((1, D_MODEL), const))
        args.append(final_g.reshape(1, -1))
    return pl.pallas_call(
        functools.partial(_post_kernel, final_norm=final_g is not None),
        grid=(b, s // TM),
        in_specs=in_specs,
        out_specs=pl.BlockSpec((1, TM, D_MODEL), row),
        out_shape=jax.ShapeDtypeStruct((b, s, D_MODEL), F32),
        compiler_params=_params(2),
        name="post_final" if final_g is not None else "post",
    )(*args)


def _fold_kernel(wf_ref, wn_ref, cc_ref, sc_ref, oa_ref, ob_ref):
    hp = lax.Precision.HIGHEST
    a = jnp.dot(cc_ref[...], wn_ref[0], precision=hp, preferred_element_type=F32)
    bm = jnp.dot(sc_ref[...], wn_ref[0], precision=hp, preferred_element_type=F32)
    oa_ref[0] = jnp.dot(wf_ref[0], a, precision=hp, preferred_element_type=F32)
    ob_ref[0] = jnp.dot(wf_ref[0], bm, precision=hp, preferred_element_type=F32)


def _channel_dft_tables(seq_len):
    n = FNET_GROUP_DIM
    jk = np.outer(np.arange(n), np.arange(n)) % n
    ang = 2.0 * np.pi * jk / n
    alpha = 1.0 / np.sqrt(float(seq_len) * n)
    cc = np.zeros((GROUP_PAD, GROUP_PAD), np.float32)
    sc = np.zeros((GROUP_PAD, GROUP_PAD), np.float32)
    cc[:n, :n] = alpha * np.cos(ang)
    sc[:n, :n] = -alpha * np.sin(ang)
    return jnp.asarray(cc), jnp.asarray(sc)


def _fold_fnet(w_in_f, w_fnet, seq_len):
    gpad = GROUP_PAD - FNET_GROUP_DIM
    wf = jnp.pad(w_in_f.reshape(D_MODEL, FNET_GROUPS, FNET_GROUP_DIM).transpose(1, 0, 2),
                 ((0, 0), (0, 0), (0, gpad)))
    wn = jnp.pad(w_fnet, ((0, 0), (0, gpad), (0, gpad)))
    cc, sc = _channel_dft_tables(seq_len)
    grp = lambda g: (g, 0, 0)
    const = lambda g: (0, 0)
    oa, ob = pl.pallas_call(
        _fold_kernel,
        grid=(FNET_GROUPS,),
        in_specs=[pl.BlockSpec((1, D_MODEL, GROUP_PAD), grp),
                  pl.BlockSpec((1, GROUP_PAD, GROUP_PAD), grp),
                  pl.BlockSpec((GROUP_PAD, GROUP_PAD), const),
                  pl.BlockSpec((GROUP_PAD, GROUP_PAD), const)],
        out_specs=[pl.BlockSpec((1, D_MODEL, GROUP_PAD), grp)] * 2,
        out_shape=[jax.ShapeDtypeStruct((FNET_GROUPS, D_MODEL, GROUP_PAD), F32)] * 2,
        compiler_params=_params(1),
        name="fnet_fold",
    )(wf, wn, cc, sc)
    unpack = lambda o: o[:, :, :FNET_GROUP_DIM].transpose(1, 0, 2).reshape(D_MODEL, MAIN_WIDTH)
    return unpack(oa), unpack(ob)


_L1_IN = 2 * MAIN_WIDTH + MEM_WIDTH + MIX_WIDTH


def _l1_pre_kernel(x_ref, g_ref, w_ref, u_ref, v_ref, qm_ref, gate_ref):
    h = _rmsnorm(x_ref[0], g_ref[...]).astype(BF16)
    proj = jnp.dot(h, w_ref[...], preferred_element_type=F32)
    u_ref[0] = proj[:, :MAIN_WIDTH].astype(BF16)
    v_ref[0] = proj[:, MAIN_WIDTH:2 * MAIN_WIDTH].astype(BF16)
    qm_ref[0] = proj[:, 2 * MAIN_WIDTH:2 * MAIN_WIDTH + MEM_WIDTH].astype(BF16)
    gate_ref[0] = proj[:, 2 * MAIN_WIDTH + MEM_WIDTH:].astype(BF16)


def _l1_pre(x, norm_g, w_all):
    b, s, _ = x.shape
    row = lambda i, j: (i, j, 0)
    const = lambda i, j: (0, 0)
    return pl.pallas_call(
        _l1_pre_kernel,
        grid=(b, s // TM),
        in_specs=[pl.BlockSpec((1, TM, D_MODEL), row),
                  pl.BlockSpec((1, D_MODEL), const),
                  pl.BlockSpec((D_MODEL, _L1_IN), const)],
        out_specs=[pl.BlockSpec((1, TM, MAIN_WIDTH), row),
                   pl.BlockSpec((1, TM, MAIN_WIDTH), row),
                   pl.BlockSpec((1, TM, MEM_WIDTH), row),
                   pl.BlockSpec((1, TM, MIX_WIDTH), row)],
        out_shape=[jax.ShapeDtypeStruct((b, s, MAIN_WIDTH), BF16),
                   jax.ShapeDtypeStruct((b, s, MAIN_WIDTH), BF16),
                   jax.ShapeDtypeStruct((b, s, MEM_WIDTH), BF16),
                   jax.ShapeDtypeStruct((b, s, MIX_WIDTH), BF16)],
        compiler_params=_params(2),
        name="l1_pre",
    )(x, norm_g.reshape(1, -1), w_all)


def _fft1_kernel(u_ref, v_ref, m1_ref, tc_ref, ts_ref, yr_ref, yi_ref):
    z = jnp.concatenate([u_ref[0], v_ref[0]], axis=0)
    y = jnp.dot(m1_ref[...].astype(BF16), z, preferred_element_type=F32)
    for j in range(FFT_N2_PER_STEP):
        sl = slice(j * MAIN_WIDTH, (j + 1) * MAIN_WIDTH)
        yr, yi = y[:FFT_R, sl], y[FFT_R:, sl]
        tc, ts = tc_ref[j], ts_ref[j]
        yr_ref[0, :, sl] = (yr * tc + yi * ts).astype(BF16)
        yi_ref[0, :, sl] = (yi * tc - yr * ts).astype(BF16)


def _fft2_kernel(yr_ref, yi_ref, g_ref, o_ref):
    g = g_ref[...].astype(BF16)
    for j in range(FFT_N2_PER_STEP):
        y = jnp.concatenate([yr_ref[0, j], yi_ref[0, j]], axis=0)
        o_ref[0, :, j * MAIN_WIDTH:(j + 1) * MAIN_WIDTH] = jnp.dot(
            g, y, preferred_element_type=F32).astype(BF16)


def _seq_dft_real(u, v):
    b, s, w = u.shape
    r = FFT_R
    assert s == r * r and w == MAIN_WIDTH
    jk = np.outer(np.arange(r), np.arange(r)) % r
    c64 = np.cos(2.0 * np.pi * jk / r)
    s64 = np.sin(2.0 * np.pi * jk / r)
    m1 = jnp.asarray(np.block([[c64, s64], [-s64, c64]]), F32)
    g2 = jnp.asarray(np.concatenate([c64, s64], axis=1), F32)
    tw = 2.0 * np.pi * np.outer(np.arange(r), np.arange(r)) / s
    tc = jnp.asarray(np.cos(tw)[:, :, None], F32)
    ts = jnp.asarray(np.sin(tw)[:, :, None], F32)

    cw = FFT_N2_PER_STEP * w
    u2 = u.reshape(b, r, r * w)
    v2 = v.reshape(b, r, r * w)
    blk = lambda c, i: (i, 0, c)
    yr, yi = pl.pallas_call(
        _fft1_kernel,
        grid=(r // FFT_N2_PER_STEP, b),
        in_specs=[pl.BlockSpec((1, r, cw), blk),
                  pl.BlockSpec((1, r, cw), blk),
                  pl.BlockSpec((2 * r, 2 * r), lambda c, i: (0, 0)),
                  pl.BlockSpec((FFT_N2_PER_STEP, r, 1), lambda c, i: (c, 0, 0)),
                  pl.BlockSpec((FFT_N2_PER_STEP, r, 1), lambda c, i: (c, 0, 0))],
        out_specs=[pl.BlockSpec((1, r, cw), blk)] * 2,
        out_shape=[jax.ShapeDtypeStruct((b, r, r * w), BF16)] * 2,
        compiler_params=_params(2),
        name="seq_dft_stage1",
    )(u2, v2, m1, tc, ts)

    yr4 = yr.reshape(b, r, r, w)
    yi4 = yi.reshape(b, r, r, w)
    blk4 = lambda i, c: (i, c, 0, 0)
    out = pl.pallas_call(
        _fft2_kernel,
        grid=(b, r // FFT_N2_PER_STEP),
        in_specs=[pl.BlockSpec((1, FFT_N2_PER_STEP, r, w), blk4),
                  pl.BlockSpec((1, FFT_N2_PER_STEP, r, w), blk4),
                  pl.BlockSpec((r, 2 * r), lambda i, c: (0, 0))],
        out_specs=pl.BlockSpec((1, r, cw), lambda i, c: (i, 0, c)),
        out_shape=jax.ShapeDtypeStruct((b, r, r * w), BF16),
        compiler_params=_params(2),
        name="seq_dft_stage2",
    )(yr4, yi4, g2)
    return out.reshape(b, s, w)


def kernel(x, mem, positions, norm_g_l0, w_in_l0, q_norm_g_l0, kv_norm_g_l0, w_uq_l0, w_ukv_l0,
           mem_norm_g_l0, w_mem_kv_l0, w_out_l0, norm_g_l1, w_in_l1, w_fnet_l1, mem_norm_g_l1,
           w_mem_kv_l1, w_out_l1, final_norm_g):
    s = x.shape[1]
    cos_a, sin_b = _rope_tables(positions)
    kbd0, vbd0 = _mem_kv(mem, mem_norm_g_l0, w_mem_kv_l0)
    kbd1, vbd1 = _mem_kv(mem, mem_norm_g_l1, w_mem_kv_l1)

    q, k, vt, qm0, gate0 = _l0_pre(x, cos_a, sin_b, norm_g_l0, w_in_l0, q_norm_g_l0, kv_norm_g_l0,
                                   w_uq_l0, w_ukv_l0)
    attn = _attention(q, k, vt)
    x1 = _post(x, attn, qm0, gate0, kbd0, vbd0, w_out_l0)

    w_a, w_b = _fold_fnet(w_in_l1[:, :MAIN_WIDTH], w_fnet_l1, s)
    w_all = jnp.concatenate([w_a, w_b, w_in_l1[:, MAIN_WIDTH:]], axis=1).astype(BF16)
    u, v, qm1, gate1 = _l1_pre(x1, norm_g_l1, w_all)
    mixed = _seq_dft_real(u, v)
    return _post(x1, mixed, qm1, gate1, kbd1, vbd1, w_out_l1, final_g=final_norm_g)
```

```python
import functools

import numpy as np
import jax
import jax.numpy as jnp
from jax import lax
from jax.experimental import pallas as pl
from jax.experimental.pallas import tpu as pltpu

F32 = jnp.float32
BF16 = jnp.bfloat16

D_MODEL = 1024
N_MEM = 256
MIX_WIDTH = D_MODEL
MEM_HEADS = 4
MEM_WIDTH = MIX_WIDTH // 4
MEM_HEAD_DIM = MEM_WIDTH // MEM_HEADS
MAIN_WIDTH = MIX_WIDTH - MEM_WIDTH
MLA_HEADS = 12
QK_NOPE = 64
QK_ROPE = 32
V_HEAD = MAIN_WIDTH // MLA_HEADS
Q_LORA = (3 * D_MODEL) // 8
KV_LORA = D_MODEL // 4
ROPE_THETA = 10000.0
FNET_GROUPS = 4
FNET_GROUP_DIM = MAIN_WIDTH // FNET_GROUPS
EPS = 1e-6

LANES = 128
HEAD_PAD = LANES
QK_PAD = MLA_HEADS * HEAD_PAD
ROPE_LO = QK_NOPE
ROPE_HALF = QK_ROPE // 2
GROUP_PAD = 256

FFT_R = 64

TM = 512
TQ = 512
TKV = 512
FFT_N2_PER_STEP = 8

VMEM_LIMIT = 56 * 1024 * 1024


def _rmsnorm(x, g):
    return x * lax.rsqrt(jnp.mean(x * x, axis=-1, keepdims=True) + EPS) * g


def _params(n_grid_dims):
    return pltpu.CompilerParams(
        dimension_semantics=("arbitrary",) * n_grid_dims,
        vmem_limit_bytes=VMEM_LIMIT,
    )


def _rope_table_kernel(pos_ref, invf_ref, cos_ref, sin_ref):
    ang = invf_ref[...] * pos_ref[0].astype(F32)
    row = lax.broadcasted_iota(jnp.int32, ang.shape, 0)
    cos_ref[0] = jnp.cos(ang)
    sin_ref[0] = jnp.where(row < ROPE_HALF, -jnp.sin(ang), jnp.sin(ang))


def _rope_tables(positions):
    b, s = positions.shape
    inv_freq = 1.0 / (ROPE_THETA ** (jnp.arange(0, QK_ROPE, 2, dtype=F32) / QK_ROPE))
    invf = jnp.concatenate([inv_freq, inv_freq]).reshape(QK_ROPE, 1)
    return pl.pallas_call(
        _rope_table_kernel,
        grid=(b,),
        in_specs=[pl.BlockSpec((1, 1, s), lambda i: (i, 0, 0)),
                  pl.BlockSpec((QK_ROPE, 1), lambda i: (0, 0))],
        out_specs=[pl.BlockSpec((1, QK_ROPE, s), lambda i: (i, 0, 0))] * 2,
        out_shape=[jax.ShapeDtypeStruct((b, QK_ROPE, s), F32)] * 2,
        compiler_params=_params(1),
        name="rope_tables",
    )(positions.reshape(b, 1, s), invf)


def _mem_kv_kernel(mem_ref, g_ref, w_ref, kbd_ref, vbd_ref):
    mn = _rmsnorm(mem_ref[0], g_ref[...]).astype(BF16)
    kv = jnp.dot(mn, w_ref[...], preferred_element_type=F32)
    k_t = (kv[:, :MEM_WIDTH] * (1.0 / float(np.sqrt(MEM_HEAD_DIM)))).T
    v = kv[:, MEM_WIDTH:]
    k_rep = jnp.concatenate([k_t] * MEM_HEADS, axis=1)
    r = lax.broadcasted_iota(jnp.int32, k_rep.shape, 0) // MEM_HEAD_DIM
    c = lax.broadcasted_iota(jnp.int32, k_rep.shape, 1) // N_MEM
    kbd_ref[0] = jnp.where(r == c, k_rep, 0.0).astype(BF16)
    v_rep = jnp.concatenate([v] * MEM_HEADS, axis=0)
    r = lax.broadcasted_iota(jnp.int32, v_rep.shape, 0) // N_MEM
    c = lax.broadcasted_iota(jnp.int32, v_rep.shape, 1) // MEM_HEAD_DIM
    vbd_ref[0] = jnp.where(r == c, v_rep, 0.0).astype(BF16)


def _mem_kv(mem, g, w_mem_kv):
    b = mem.shape[0]
    return pl.pallas_call(
        _mem_kv_kernel,
        grid=(b,),
        in_specs=[pl.BlockSpec((1, N_MEM, D_MODEL), lambda i: (i, 0, 0)),
                  pl.BlockSpec((1, D_MODEL), lambda i: (0, 0)),
                  pl.BlockSpec((D_MODEL, 2 * MEM_WIDTH), lambda i: (0, 0))],
        out_specs=[pl.BlockSpec((1, MEM_WIDTH, MEM_HEADS * N_MEM), lambda i: (i, 0, 0)),
                   pl.BlockSpec((1, MEM_HEADS * N_MEM, MEM_WIDTH), lambda i: (i, 0, 0))],
        out_shape=[jax.ShapeDtypeStruct((b, MEM_WIDTH, MEM_HEADS * N_MEM), BF16),
                   jax.ShapeDtypeStruct((b, MEM_HEADS * N_MEM, MEM_WIDTH), BF16)],
        compiler_params=_params(1),
        name="mem_kv",
    )(mem, g.reshape(1, D_MODEL), w_mem_kv.astype(BF16))


_L0_CQ = (0, Q_LORA)
_L0_CKV = (_L0_CQ[1], _L0_CQ[1] + KV_LORA)
_L0_QM = (_L0_CKV[1], _L0_CKV[1] + MEM_WIDTH)
_L0_GATE = (_L0_QM[1], _L0_QM[1] + MIX_WIDTH)
_L0_KR = (_L0_GATE[1], _L0_GATE[1] + HEAD_PAD)
_L0_IN = _L0_KR[1]


def _l0_pre_kernel(x_ref, cos_ref, sin_ref, g_ref, w_in_ref, gq_ref, gkv_ref, wqt_ref, wk_ref,
                   wvt_ref, qt_ref, k_ref, vt_ref, qm_ref, gate_ref):
    h = _rmsnorm(x_ref[0], g_ref[...]).astype(BF16)
    proj = jnp.dot(h, w_in_ref[...], preferred_element_type=F32)
    qm_ref[0] = proj[:, _L0_QM[0]:_L0_QM[1]].astype(BF16)
    gate_ref[0] = proj[:, _L0_GATE[0]:_L0_GATE[1]].astype(BF16)
    cqn = _rmsnorm(proj[:, _L0_CQ[0]:_L0_CQ[1]], gq_ref[...]).astype(BF16)
    ckvn = _rmsnorm(proj[:, _L0_CKV[0]:_L0_CKV[1]], gkv_ref[...]).astype(BF16)
    nt = (((1,), (1,)), ((), ()))
    qt = lax.dot_general(wqt_ref[...], cqn, nt, preferred_element_type=F32)
    kn = jnp.dot(ckvn, wk_ref[...], preferred_element_type=F32)
    vt = lax.dot_general(wvt_ref[...], ckvn, nt, preferred_element_type=F32)
    vt_ref[0] = vt.astype(BF16)

    cos_t = cos_ref[0]
    sin_t = sin_ref[0]

    def rope_t(blk):
        partner = jnp.concatenate([blk[ROPE_HALF:], blk[:ROPE_HALF]], axis=0)
        return blk * cos_t + partner * sin_t

    rope_rows = slice(ROPE_LO, ROPE_LO + QK_ROPE)
    kr_t = proj[:, _L0_KR[0]:_L0_KR[1]].T
    k_rope = jnp.concatenate([kr_t[:ROPE_LO], rope_t(kr_t[rope_rows]), kr_t[ROPE_LO + QK_ROPE:]],
                             axis=0).T
    scale = float(np.log2(np.e)) / float(np.sqrt(QK_NOPE + QK_ROPE))
    for hd in range(MLA_HEADS):
        sl = slice(hd * HEAD_PAD, (hd + 1) * HEAD_PAD)
        q_h = qt[sl]
        q_h = jnp.concatenate([q_h[:ROPE_LO], rope_t(q_h[rope_rows]), q_h[ROPE_LO + QK_ROPE:]], axis=0)
        qt_ref[0, sl, :] = (q_h * scale).astype(BF16)
        k_ref[0, :, sl] = (kn[:, sl] + k_rope).astype(BF16)


def _l0_pre(x, cos_t, sin_t, norm_g, w_in, q_norm_g, kv_norm_g, w_uq, w_ukv):
    b, s, _ = x.shape
    o1, o2, o3, o4 = Q_LORA, Q_LORA + KV_LORA, Q_LORA + KV_LORA + QK_ROPE, Q_LORA + KV_LORA + QK_ROPE + MEM_WIDTH
    w_kr = jnp.pad(w_in[:, o2:o3], ((0, 0), (ROPE_LO, HEAD_PAD - ROPE_LO - QK_ROPE)))
    w_in_p = jnp.concatenate([w_in[:, :o2], w_in[:, o3:o4], w_in[:, o4:], w_kr], axis=1).astype(BF16)
    wqt = jnp.pad(w_uq.reshape(Q_LORA, MLA_HEADS, QK_NOPE + QK_ROPE),
                  ((0, 0), (0, 0), (0, HEAD_PAD - QK_NOPE - QK_ROPE))).reshape(Q_LORA, QK_PAD).T.astype(BF16)
    w_ukv3 = w_ukv.reshape(KV_LORA, MLA_HEADS, QK_NOPE + V_HEAD)
    wk = jnp.pad(w_ukv3[:, :, :QK_NOPE], ((0, 0), (0, 0), (0, HEAD_PAD - QK_NOPE))
                 ).reshape(KV_LORA, QK_PAD).astype(BF16)
    wvt = w_ukv3[:, :, QK_NOPE:].reshape(KV_LORA, MAIN_WIDTH).T.astype(BF16)

    row = lambda i, j: (i, j, 0)
    col = lambda i, j: (i, 0, j)
    const = lambda i, j: (0, 0)
    return pl.pallas_call(
        _l0_pre_kernel,
        grid=(b, s // TM),
        in_specs=[pl.BlockSpec((1, TM, D_MODEL), row),
                  pl.BlockSpec((1, QK_ROPE, TM), col),
                  pl.BlockSpec((1, QK_ROPE, TM), col),
                  pl.BlockSpec((1, D_MODEL), const),
                  pl.BlockSpec((D_MODEL, _L0_IN), const),
                  pl.BlockSpec((1, Q_LORA), const),
                  pl.BlockSpec((1, KV_LORA), const),
                  pl.BlockSpec((QK_PAD, Q_LORA), const),
                  pl.BlockSpec((KV_LORA, QK_PAD), const),
                  pl.BlockSpec((MAIN_WIDTH, KV_LORA), const)],
        out_specs=[pl.BlockSpec((1, QK_PAD, TM), col),
                   pl.BlockSpec((1, TM, QK_PAD), row),
                   pl.BlockSpec((1, MAIN_WIDTH, TM), col),
                   pl.BlockSpec((1, TM, MEM_WIDTH), row),
                   pl.BlockSpec((1, TM, MIX_WIDTH), row)],
        out_shape=[jax.ShapeDtypeStruct((b, QK_PAD, s), BF16),
                   jax.ShapeDtypeStruct((b, s, QK_PAD), BF16),
                   jax.ShapeDtypeStruct((b, MAIN_WIDTH, s), BF16),
                   jax.ShapeDtypeStruct((b, s, MEM_WIDTH), BF16),
                   jax.ShapeDtypeStruct((b, s, MIX_WIDTH), BF16)],
        compiler_params=_params(2),
        name="l0_pre",
    )(x, cos_t, sin_t, norm_g.reshape(1, -1), w_in_p, q_norm_g.reshape(1, -1),
      kv_norm_g.reshape(1, -1), wqt, wk, wvt)


HEADS_PER_STEP = 2
SUM_ROWS = 16


def _attn_kernel(qt_ref, k_ref, vt_ref, o_ref, s_ref, mx_ref, m_ref, acc_ref):
    t = pl.program_id(2)
    n_tiles = pl.num_programs(2)
    n_chunks = k_ref.shape[1] // TKV
    assert n_chunks % 2 == 0

    def scores(tile, chunk, slot):
        q_off = pl.multiple_of(tile * TQ, TQ)
        k_off = pl.multiple_of(chunk * TKV, TKV)
        for hd in range(HEADS_PER_STEP):
            lanes = slice(hd * HEAD_PAD, (hd + 1) * HEAD_PAD)
            s_t = jnp.dot(k_ref[0, pl.ds(k_off, TKV), lanes], qt_ref[0, lanes, pl.ds(q_off, TQ)],
                          preferred_element_type=F32)
            s_ref[slot, hd] = s_t
            mx_ref[slot, hd] = jnp.max(s_t, axis=0, keepdims=True)

    def consume(chunk, slot):
        k_off = pl.multiple_of(chunk * TKV, TKV)
        ones = jnp.ones((SUM_ROWS, TKV), BF16)
        for hd in range(HEADS_PER_STEP):
            m_old = m_ref[hd]
            m_new = jnp.maximum(m_old, mx_ref[slot, hd])
            alpha = jnp.exp2(m_old - m_new)
            p = jnp.exp2(s_ref[slot, hd] - m_new).astype(BF16)
            m_ref[hd] = m_new
            v_aug = jnp.concatenate(
                [vt_ref[0, hd * V_HEAD:(hd + 1) * V_HEAD, pl.ds(k_off, TKV)], ones], axis=0)
            acc_ref[hd] = alpha * acc_ref[hd] + jnp.dot(v_aug, p, preferred_element_type=F32)

    @pl.when(t == 0)
    def _():
        scores(0, 0, 0)

    m_ref[...] = jnp.full(m_ref.shape, -jnp.inf, F32)
    acc_ref[...] = jnp.zeros(acc_ref.shape, F32)

    for c in range(n_chunks):
        if c + 1 < n_chunks:
            scores(t, c + 1, (c + 1) % 2)
        else:
            scores(jnp.minimum(t + 1, n_tiles - 1), 0, (c + 1) % 2)
        consume(c, c % 2)
    out =[acc_ref[hd, :V_HEAD, :] * (1.0 / acc_ref[hd, V_HEAD:V_HEAD + 1, :])
           for hd in range(HEADS_PER_STEP)]
    o_ref[0] = jnp.concatenate(out, axis=0).T.astype(BF16)


def _attention(qt, k, vt):
    b, s, _ = k.shape
    width = HEADS_PER_STEP * HEAD_PAD
    vrows = HEADS_PER_STEP * V_HEAD
    return pl.pallas_call(
        _attn_kernel,
        grid=(b, MLA_HEADS // HEADS_PER_STEP, s // TQ),
        in_specs=[pl.BlockSpec((1, width, s), lambda i, j, t: (i, j, 0)),
                  pl.BlockSpec((1, s, width), lambda i, j, t: (i, 0, j)),
                  pl.BlockSpec((1, vrows, s), lambda i, j, t: (i, j, 0))],
        out_specs=pl.BlockSpec((1, TQ, vrows), lambda i, j, t: (i, t, j)),
        out_shape=jax.ShapeDtypeStruct((b, s, MAIN_WIDTH), BF16),
        scratch_shapes=[pltpu.VMEM((2, HEADS_PER_STEP, TKV, TQ), F32),
                        pltpu.VMEM((2, HEADS_PER_STEP, 1, TQ), F32),
                        pltpu.VMEM((HEADS_PER_STEP, 1, TQ), F32),
                        pltpu.VMEM((HEADS_PER_STEP, V_HEAD + SUM_ROWS, TQ), F32)],
        compiler_params=_params(3),
        name="mla_attention",
    )(qt, k, vt)


def _post_kernel(*refs, final_norm):
    if final_norm:
        x_ref, main_ref, qm_ref, gate_ref, kbd_ref, vbd_ref, wout_ref, fg_ref, o_ref = refs
    else:
        x_ref, main_ref, qm_ref, gate_ref, kbd_ref, vbd_ref, wout_ref, o_ref = refs
    s = jnp.dot(qm_ref[0], kbd_ref[0], preferred_element_type=F32)
    probs = []
    for hd in range(MEM_HEADS):
        sh = s[:, hd * N_MEM:(hd + 1) * N_MEM]
        e = jnp.exp(sh - jnp.max(sh, axis=-1, keepdims=True))
        probs.append((e / jnp.sum(e, axis=-1, keepdims=True)).astype(BF16))
    mem_out = jnp.dot(jnp.concatenate(probs, axis=-1), vbd_ref[0], preferred_element_type=F32)
    gate = gate_ref[0].astype(F32)
    branch = jnp.concatenate([main_ref[0].astype(F32), mem_out], axis=-1) * (gate * jax.nn.sigmoid(gate))
    y = x_ref[0] + jnp.dot(branch.astype(BF16), wout_ref[...], preferred_element_type=F32)
    if final_norm:
        y = _rmsnorm(y, fg_ref[...])
    o_ref[0] = y


def _post(x, main, qm, gate, kbd, vbd, w_out, final_g=None):
    b, s, _ = x.shape
    row = lambda i, j: (i, j, 0)
    per_batch = lambda i, j: (i, 0, 0)
    const = lambda i, j: (0, 0)
    in_specs = [pl.BlockSpec((1, TM, D_MODEL), row),
                pl.BlockSpec((1, TM, MAIN_WIDTH), row),
                pl.BlockSpec((1, TM, MEM_WIDTH), row),
                pl.BlockSpec((1, TM, MIX_WIDTH), row),
                pl.BlockSpec((1, MEM_WIDTH, MEM_HEADS * N_MEM), per_batch),
                pl.BlockSpec((1, MEM_HEADS * N_MEM, MEM_WIDTH), per_batch),
                pl.BlockSpec((MIX_WIDTH, D_MODEL), const)]
    args = [x, main, qm, gate, kbd, vbd, w_out.astype(BF16)]
    if final_g is not None:
        in_specs.append(pl.BlockSpec((1, D_MODEL), const))
        args.append(final_g.reshape(1, -1))
    return pl.pallas_call(
        functools.partial(_post_kernel, final_norm=final_g is not None),
        grid=(b, s // TM),
        in_specs=in_specs,
        out_specs=pl.BlockSpec((1, TM, D_MODEL), row),
        out_shape=jax.ShapeDtypeStruct((b, s, D_MODEL), F32),
        compiler_params=_params(2),
        name="post_final" if final_g is not None else "post",
    )(*args)


def _fold_kernel(wf_ref, wn_ref, cc_ref, sc_ref, oa_ref, ob_ref):
    hp = lax.Precision.HIGHEST
    a = jnp.dot(cc_ref[...], wn_ref[0], precision=hp, preferred_element_type=F32)
    bm = jnp.dot(sc_ref[...], wn_ref[0], precision=hp, preferred_element_type=F32)
    oa_ref[0] = jnp.dot(wf_ref[0], a, precision=hp, preferred_element_type=F32)
    ob_ref[0] = jnp.dot(wf_ref[0], bm, precision=hp, preferred_element_type=F32)


def _channel_dft_tables(seq_len):
    n = FNET_GROUP_DIM
    jk = np.outer(np.arange(n), np.arange(n)) % n
    ang = 2.0 * np.pi * jk / n
    alpha = 1.0 / np.sqrt(float(seq_len) * n)
    cc = np.zeros((GROUP_PAD, GROUP_PAD), np.float32)
    sc = np.zeros((GROUP_PAD, GROUP_PAD), np.float32)
    cc[:n, :n] = alpha * np.cos(ang)
    sc[:n, :n] = -alpha * np.sin(ang)
    return jnp.asarray(cc), jnp.asarray(sc)


def _fold_fnet(w_in_f, w_fnet, seq_len):
    gpad = GROUP_PAD - FNET_GROUP_DIM
    wf = jnp.pad(w_in_f.reshape(D_MODEL, FNET_GROUPS, FNET_GROUP_DIM).transpose(1, 0, 2),
                 ((0, 0), (0, 0), (0, gpad)))
    wn = jnp.pad(w_fnet, ((0, 0), (0, gpad), (0, gpad)))
    cc, sc = _channel_dft_tables(seq_len)
    grp = lambda g: (g, 0, 0)
    const = lambda g: (0, 0)
    oa, ob = pl.pallas_call(
        _fold_kernel,
        grid=(FNET_GROUPS,),
        in_specs=[pl.BlockSpec((1, D_MODEL, GROUP_PAD), grp),
                  pl.BlockSpec((1, GROUP_PAD, GROUP_PAD), grp),
                  pl.BlockSpec((GROUP_PAD, GROUP_PAD), const),
                  pl.BlockSpec((GROUP_PAD, GROUP_PAD), const)],
        out_specs=[pl.BlockSpec((1, D_MODEL, GROUP_PAD), grp)] * 2,
        out_shape=[jax.ShapeDtypeStruct((FNET_GROUPS, D_MODEL, GROUP_PAD), F32)] * 2,
        compiler_params=_params(1),
        name="fnet_fold",
    )(wf, wn, cc, sc)
    unpack = lambda o: o[:, :, :FNET_GROUP_DIM].transpose(1, 0, 2).reshape(D_MODEL, MAIN_WIDTH)
    return unpack(oa), unpack(ob)


_L1_IN = 2 * MAIN_WIDTH + MEM_WIDTH + MIX_WIDTH


def _l1_pre_kernel(x_ref, g_ref, w_ref, u_ref, v_ref, qm_ref, gate_ref):
    h = _rmsnorm(x_ref[0], g_ref[...]).astype(BF16)
    proj = jnp.dot(h, w_ref[...], preferred_element_type=F32)
    u_ref[0] = proj[:, :MAIN_WIDTH].astype(BF16)
    v_ref[0] = proj[:, MAIN_WIDTH:2 * MAIN_WIDTH].astype(BF16)
    qm_ref[0] = proj[:, 2 * MAIN_WIDTH:2 * MAIN_WIDTH + MEM_WIDTH].astype(BF16)
    gate_ref[0] = proj[:, 2 * MAIN_WIDTH + MEM_WIDTH:].astype(BF16)


def _l1_pre(x, norm_g, w_all):
    b, s, _ = x.shape
    row = lambda i, j: (i, j, 0)
    const = lambda i, j: (0, 0)
    return pl.pallas_call(
        _l1_pre_kernel,
        grid=(b, s // TM),
        in_specs=[pl.BlockSpec((1, TM, D_MODEL), row),
                  pl.BlockSpec((1, D_MODEL), const),
                  pl.BlockSpec((D_MODEL, _L1_IN), const)],
        out_specs=[pl.BlockSpec((1, TM, MAIN_WIDTH), row),
                   pl.BlockSpec((1, TM, MAIN_WIDTH), row),
                   pl.BlockSpec((1, TM, MEM_WIDTH), row),
                   pl.BlockSpec((1, TM, MIX_WIDTH), row)],
        out_shape=[jax.ShapeDtypeStruct((b, s, MAIN_WIDTH), BF16),
                   jax.ShapeDtypeStruct((b, s, MAIN_WIDTH), BF16),
                   jax.ShapeDtypeStruct((b, s, MEM_WIDTH), BF16),
                   jax.ShapeDtypeStruct((b, s, MIX_WIDTH), BF16)],
        compiler_params=_params(2),
        name="l1_pre",
    )(x, norm_g.reshape(1, -1), w_all)


def _fft1_kernel(u_ref, v_ref, m1_ref, tc_ref, ts_ref, yr_ref, yi_ref):
    z = jnp.concatenate([u_ref[0], v_ref[0]], axis=0)
    y = jnp.dot(m1_ref[...].astype(BF16), z, preferred_element_type=F32)
    for j in range(FFT_N2_PER_STEP):
        sl = slice(j * MAIN_WIDTH, (j + 1) * MAIN_WIDTH)
        yr, yi = y[:FFT_R, sl], y[FFT_R:, sl]
        tc, ts = tc_ref[j], ts_ref[j]
        yr_ref[0, :, sl] = (yr * tc + yi * ts).astype(BF16)
        yi_ref[0, :, sl] = (yi * tc - yr * ts).astype(BF16)


def _fft2_kernel(yr_ref, yi_ref, g_ref, o_ref):
    g = g_ref[...].astype(BF16)
    for j in range(FFT_N2_PER_STEP):
        y = jnp.concatenate([yr_ref[0, j], yi_ref[0, j]], axis=0)
        o_ref[0, :, j * MAIN_WIDTH:(j + 1) * MAIN_WIDTH] = jnp.dot(
            g, y, preferred_element_type=F32).astype(BF16)


def _seq_dft_real(u, v):
    b, s, w = u.shape
    r = FFT_R
    assert s == r * r and w == MAIN_WIDTH
    jk = np.outer(np.arange(r), np.arange(r)) % r
    c64 = np.cos(2.0 * np.pi * jk / r)
    s64 = np.sin(2.0 * np.pi * jk / r)
    m1 = jnp.asarray(np.block([[c64, s64], [-s64, c64]]), F32)
    g2 = jnp.asarray(np.concatenate([c64, s64], axis=1), F32)
    tw = 2.0 * np.pi * np.outer(np.arange(r), np.arange(r)) / s
    tc = jnp.asarray(np.cos(tw)[:, :, None], F32)
    ts = jnp.asarray(np.sin(tw)[:, :, None], F32)

    cw = FFT_N2_PER_STEP * w
    u2 = u.reshape(b, r, r * w)
    v2 = v.reshape(b, r, r * w)
    blk = lambda c, i: (i, 0, c)
    yr, yi = pl.pallas_call(
        _fft1_kernel,
        grid=(r // FFT_N2_PER_STEP, b),
        in_specs=[pl.BlockSpec((1, r, cw), blk),
                  pl.BlockSpec((1, r, cw), blk),
                  pl.BlockSpec((2 * r, 2 * r), lambda c, i: (0, 0)),
                  pl.BlockSpec((FFT_N2_PER_STEP, r, 1), lambda c, i: (c, 0, 0)),
                  pl.BlockSpec((FFT_N2_PER_STEP, r, 1), lambda c, i: (c, 0, 0))],
        out_specs=[pl.BlockSpec((1, r, cw), blk)] * 2,
        out_shape=[jax.ShapeDtypeStruct((b, r, r * w), BF16)] * 2,
        compiler_params=_params(2),
        name="seq_dft_stage1",
    )(u2, v2, m1, tc, ts)

    yr4 = yr.reshape(b, r, r, w)
    yi4 = yi.reshape(b, r, r, w)
    blk4 = lambda i, c: (i, c, 0, 0)
    out = pl.pallas_call(
        _fft2_kernel,
        grid=(b, r // FFT_N2_PER_STEP),
        in_specs=[pl.BlockSpec((1, FFT_N2_PER_STEP, r, w), blk4),
                  pl.BlockSpec((1, FFT_N2_PER_STEP, r, w), blk4),
                  pl.BlockSpec((r, 2 * r), lambda i, c: (0, 0))],
        out_specs=pl.BlockSpec((1, r, cw), lambda i, c: (i, 0, c)),
        out_shape=jax.ShapeDtypeStruct((b, r, r * w), BF16),
        compiler_params=_params(2),
        name="seq_dft_stage2",
    )(yr4, yi4, g2)
    return out.reshape(b, s, w)


def kernel(x, mem, positions, norm_g_l0, w_in_l0, q_norm_g_l0, kv_norm_g_l0, w_uq_l0, w_ukv_l0,
           mem_norm_g_l0, w_mem_kv_l0, w_out_l0, norm_g_l1, w_in_l1, w_fnet_l1, mem_norm_g_l1,
           w_mem_kv_l1, w_out_l1, final_norm_g):
    s = x.shape[1]
    cos_t, sin_t = _rope_tables(positions)
    kbd0, vbd0 = _mem_kv(mem, mem_norm_g_l0, w_mem_kv_l0)
    kbd1, vbd1 = _mem_kv(mem, mem_norm_g_l1, w_mem_kv_l1)

    qt, k, vt, qm0, gate0 = _l0_pre(x, cos_t, sin_t, norm_g_l0, w_in_l0, q_norm_g_l0, kv_norm_g_l0,
                                    w_uq_l0, w_ukv_l0)
    attn = _attention(qt, k, vt)
    x1 = _post(x, attn, qm0, gate0, kbd0, vbd0, w_out_l0)

    w_a, w_b = _fold_fnet(w_in_l1[:, :MAIN_WIDTH], w_fnet_l1, s)
    w_all = jnp.concatenate([w_a, w_b, w_in_l1[:, MAIN_WIDTH:]], axis=1).astype(BF16)
    u, v, qm1, gate1 = _l1_pre(x1, norm_g_l1, w_all)
    mixed = _seq_dft_real(u, v)
    return _post(x1, mixed, qm1, gate1, kbd1, vbd1, w_out_l1, final_g=final_norm_g)
```

```python
import functools

import numpy as np
import jax
import jax.numpy as jnp
from jax import lax
from jax.experimental import pallas as pl
from jax.experimental.pallas import tpu as pltpu

F32 = jnp.float32
BF16 = jnp.bfloat16

D_MODEL = 1024
N_MEM = 256
MIX_WIDTH = D_MODEL
MEM_HEADS = 4
MEM_WIDTH = MIX_WIDTH // 4
MEM_HEAD_DIM = MEM_WIDTH // MEM_HEADS
MAIN_WIDTH = MIX_WIDTH - MEM_WIDTH
MLA_HEADS = 12
QK_NOPE = 64
QK_ROPE = 32
V_HEAD = MAIN_WIDTH // MLA_HEADS
Q_LORA = (3 * D_MODEL) // 8
KV_LORA = D_MODEL // 4
ROPE_THETA = 10000.0
FNET_GROUPS = 4
FNET_GROUP_DIM = MAIN_WIDTH // FNET_GROUPS
EPS = 1e-6

LANES = 128
HEAD_PAD = LANES
QK_PAD = MLA_HEADS * HEAD_PAD
ROPE_LO = QK_NOPE
ROPE_HALF = QK_ROPE // 2
GROUP_PAD = 256

FFT_R = 16
FFT_G = FFT_R * FFT_R
FFT_CW = 256
FFT_UNROLL = 8

TM = 512
TQ = 512
TKV = 256
TM_POST = 1024
POST_SUBTILES = 4

VMEM_LIMIT = 56 * 1024 * 1024


def _rmsnorm(x, g):
    return x * lax.rsqrt(jnp.mean(x * x, axis=-1, keepdims=True) + EPS) * g


def _params(n_grid_dims):
    return pltpu.CompilerParams(
        dimension_semantics=("arbitrary",) * n_grid_dims,
        vmem_limit_bytes=VMEM_LIMIT,
    )


def _rope_table_kernel(pos_ref, invf_ref, cos_ref, sin_ref):
    ang = invf_ref[...] * pos_ref[0].astype(F32)
    row = lax.broadcasted_iota(jnp.int32, ang.shape, 0)
    cos_ref[0] = jnp.cos(ang)
    sin_ref[0] = jnp.where(row < ROPE_HALF, -jnp.sin(ang), jnp.sin(ang))


def _rope_tables(positions):
    b, s = positions.shape
    inv_freq = 1.0 / (ROPE_THETA ** (jnp.arange(0, QK_ROPE, 2, dtype=F32) / QK_ROPE))
    invf = jnp.concatenate([inv_freq, inv_freq]).reshape(QK_ROPE, 1)
    return pl.pallas_call(
        _rope_table_kernel,
        grid=(b,),
        in_specs=[pl.BlockSpec((1, 1, s), lambda i: (i, 0, 0)),
                  pl.BlockSpec((QK_ROPE, 1), lambda i: (0, 0))],
        out_specs=[pl.BlockSpec((1, QK_ROPE, s), lambda i: (i, 0, 0))] * 2,
        out_shape=[jax.ShapeDtypeStruct((b, QK_ROPE, s), F32)] * 2,
        compiler_params=_params(1),
        name="rope_tables",
    )(positions.reshape(b, 1, s), invf)


def _mem_kv_kernel(mem_ref, g_ref, w_ref, kbd_ref, vbd_ref):
    mn = _rmsnorm(mem_ref[0], g_ref[...]).astype(BF16)
    kv = jnp.dot(mn, w_ref[...], preferred_element_type=F32)
    k_t = (kv[:, :MEM_WIDTH] * (1.0 / float(np.sqrt(MEM_HEAD_DIM)))).T
    v = kv[:, MEM_WIDTH:]
    k_rep = jnp.concatenate([k_t] * MEM_HEADS, axis=1)
    r = lax.broadcasted_iota(jnp.int32, k_rep.shape, 0) // MEM_HEAD_DIM
    c = lax.broadcasted_iota(jnp.int32, k_rep.shape, 1) // N_MEM
    kbd_ref[0] = jnp.where(r == c, k_rep, 0.0).astype(BF16)
    v_rep = jnp.concatenate([v] * MEM_HEADS, axis=0)
    r = lax.broadcasted_iota(jnp.int32, v_rep.shape, 0) // N_MEM
    c = lax.broadcasted_iota(jnp.int32, v_rep.shape, 1) // MEM_HEAD_DIM
    vbd_ref[0] = jnp.where(r == c, v_rep, 0.0).astype(BF16)


def _mem_kv(mem, g, w_mem_kv):
    b = mem.shape[0]
    return pl.pallas_call(
        _mem_kv_kernel,
        grid=(b,),
        in_specs=[pl.BlockSpec((1, N_MEM, D_MODEL), lambda i: (i, 0, 0)),
                  pl.BlockSpec((1, D_MODEL), lambda i: (0, 0)),
                  pl.BlockSpec((D_MODEL, 2 * MEM_WIDTH), lambda i: (0, 0))],
        out_specs=[pl.BlockSpec((1, MEM_WIDTH, MEM_HEADS * N_MEM), lambda i: (i, 0, 0)),
                   pl.BlockSpec((1, MEM_HEADS * N_MEM, MEM_WIDTH), lambda i: (i, 0, 0))],
        out_shape=[jax.ShapeDtypeStruct((b, MEM_WIDTH, MEM_HEADS * N_MEM), BF16),
                   jax.ShapeDtypeStruct((b, MEM_HEADS * N_MEM, MEM_WIDTH), BF16)],
        compiler_params=_params(1),
        name="mem_kv",
    )(mem, g.reshape(1, D_MODEL), w_mem_kv.astype(BF16))


_L0_CQ = (0, Q_LORA)
_L0_CKV = (_L0_CQ[1], _L0_CQ[1] + KV_LORA)
_L0_QM = (_L0_CKV[1], _L0_CKV[1] + MEM_WIDTH)
_L0_GATE = (_L0_QM[1], _L0_QM[1] + MIX_WIDTH)
_L0_KR = (_L0_GATE[1], _L0_GATE[1] + HEAD_PAD)
_L0_IN = _L0_KR[1]


def _l0_pre_kernel(x_ref, cos_ref, sin_ref, g_ref, w_in_ref, gq_ref, gkv_ref, wqt_ref, wk_ref,
                   wvt_ref, qt_ref, k_ref, vt_ref, qm_ref, gate_ref):
    h = _rmsnorm(x_ref[0], g_ref[...]).astype(BF16)
    proj = jnp.dot(h, w_in_ref[...], preferred_element_type=F32)
    qm_ref[0] = proj[:, _L0_QM[0]:_L0_QM[1]].astype(BF16)
    gate_ref[0] = proj[:, _L0_GATE[0]:_L0_GATE[1]].astype(BF16)
    cqn = _rmsnorm(proj[:, _L0_CQ[0]:_L0_CQ[1]], gq_ref[...]).astype(BF16)
    ckvn = _rmsnorm(proj[:, _L0_CKV[0]:_L0_CKV[1]], gkv_ref[...]).astype(BF16)
    nt = (((1,), (1,)), ((), ()))
    qt = lax.dot_general(wqt_ref[...], cqn, nt, preferred_element_type=F32)
    kn = jnp.dot(ckvn, wk_ref[...], preferred_element_type=F32)
    vt = lax.dot_general(wvt_ref[...], ckvn, nt, preferred_element_type=F32)
    vt_ref[0] = vt.astype(BF16)

    cos_t = cos_ref[0]
    sin_t = sin_ref[0]

    def rope_t(blk):
        partner = jnp.concatenate([blk[ROPE_HALF:], blk[:ROPE_HALF]], axis=0)
        return blk * cos_t + partner * sin_t

    rope_rows = slice(ROPE_LO, ROPE_LO + QK_ROPE)
    kr_t = proj[:, _L0_KR[0]:_L0_KR[1]].T
    k_rope = jnp.concatenate([kr_t[:ROPE_LO], rope_t(kr_t[rope_rows]), kr_t[ROPE_LO + QK_ROPE:]],
                             axis=0).T
    scale = float(np.log2(np.e)) / float(np.sqrt(QK_NOPE + QK_ROPE))
    for hd in range(MLA_HEADS):
        sl = slice(hd * HEAD_PAD, (hd + 1) * HEAD_PAD)
        q_h = qt[sl]
        q_h = jnp.concatenate([q_h[:ROPE_LO], rope_t(q_h[rope_rows]), q_h[ROPE_LO + QK_ROPE:]], axis=0)
        qt_ref[0, sl, :] = (q_h * scale).astype(BF16)
        k_ref[0, :, sl] = (kn[:, sl] + k_rope).astype(BF16)


def _l0_pre(x, cos_t, sin_t, norm_g, w_in, q_norm_g, kv_norm_g, w_uq, w_ukv):
    b, s, _ = x.shape
    o1, o2, o3, o4 = Q_LORA, Q_LORA + KV_LORA, Q_LORA + KV_LORA + QK_ROPE, Q_LORA + KV_LORA + QK_ROPE + MEM_WIDTH
    w_kr = jnp.pad(w_in[:, o2:o3], ((0, 0), (ROPE_LO, HEAD_PAD - ROPE_LO - QK_ROPE)))
    w_in_p = jnp.concatenate([w_in[:, :o2], w_in[:, o3:o4], w_in[:, o4:], w_kr], axis=1).astype(BF16)
    wqt = jnp.pad(w_uq.reshape(Q_LORA, MLA_HEADS, QK_NOPE + QK_ROPE),
                  ((0, 0), (0, 0), (0, HEAD_PAD - QK_NOPE - QK_ROPE))).reshape(Q_LORA, QK_PAD).T.astype(BF16)
    w_ukv3 = w_ukv.reshape(KV_LORA, MLA_HEADS, QK_NOPE + V_HEAD)
    wk = jnp.pad(w_ukv3[:, :, :QK_NOPE], ((0, 0), (0, 0), (0, HEAD_PAD - QK_NOPE))
                 ).reshape(KV_LORA, QK_PAD).astype(BF16)
    wvt = w_ukv3[:, :, QK_NOPE:].reshape(KV_LORA, MAIN_WIDTH).T.astype(BF16)

    row = lambda i, j: (i, j, 0)
    col = lambda i, j: (i, 0, j)
    const = lambda i, j: (0, 0)
    return pl.pallas_call(
        _l0_pre_kernel,
        grid=(b, s // TM),
        in_specs=[pl.BlockSpec((1, TM, D_MODEL), row),
                  pl.BlockSpec((1, QK_ROPE, TM), col),
                  pl.BlockSpec((1, QK_ROPE, TM), col),
                  pl.BlockSpec((1, D_MODEL), const),
                  pl.BlockSpec((D_MODEL, _L0_IN), const),
                  pl.BlockSpec((1, Q_LORA), const),
                  pl.BlockSpec((1, KV_LORA), const),
                  pl.BlockSpec((QK_PAD, Q_LORA), const),
                  pl.BlockSpec((KV_LORA, QK_PAD), const),
                  pl.BlockSpec((MAIN_WIDTH, KV_LORA), const)],
        out_specs=[pl.BlockSpec((1, QK_PAD, TM), col),
                   pl.BlockSpec((1, TM, QK_PAD), row),
                   pl.BlockSpec((1, MAIN_WIDTH, TM), col),
                   pl.BlockSpec((1, TM, MEM_WIDTH), row),
                   pl.BlockSpec((1, TM, MIX_WIDTH), row)],
        out_shape=[jax.ShapeDtypeStruct((b, QK_PAD, s), BF16),
                   jax.ShapeDtypeStruct((b, s, QK_PAD), BF16),
                   jax.ShapeDtypeStruct((b, MAIN_WIDTH, s), BF16),
                   jax.ShapeDtypeStruct((b, s, MEM_WIDTH), BF16),
                   jax.ShapeDtypeStruct((b, s, MIX_WIDTH), BF16)],
        compiler_params=_params(2),
        name="l0_pre",
    )(x, cos_t, sin_t, norm_g.reshape(1, -1), w_in_p, q_norm_g.reshape(1, -1),
      kv_norm_g.reshape(1, -1), wqt, wk, wvt)


HEADS_PER_STEP = 2
SUM_ROWS = 16


def _attn_kernel(qt_ref, k_ref, vt_ref, o_ref, s_ref, mx_ref, m_ref, acc_ref):
    t = pl.program_id(2)
    n_tiles = pl.num_programs(2)
    n_chunks = k_ref.shape[1] // TKV
    assert n_chunks % 2 == 0

    def scores(tile, chunk, slot):
        q_off = pl.multiple_of(tile * TQ, TQ)
        k_off = pl.multiple_of(chunk * TKV, TKV)
        for hd in range(HEADS_PER_STEP):
            lanes = slice(hd * HEAD_PAD, (hd + 1) * HEAD_PAD)
            s_t = jnp.dot(k_ref[0, pl.ds(k_off, TKV), lanes], qt_ref[0, lanes, pl.ds(q_off, TQ)],
                          preferred_element_type=F32)
            s_ref[slot, hd] = s_t
            mx_ref[slot, hd] = jnp.max(s_t, axis=0, keepdims=True)

    def consume(chunk, slot):
        k_off = pl.multiple_of(chunk * TKV, TKV)
        ones = jnp.ones((SUM_ROWS, TKV), BF16)
        for hd in range(HEADS_PER_STEP):
            m_old = m_ref[hd]
            m_new = jnp.maximum(m_old, mx_ref[slot, hd])
            alpha = jnp.exp2(m_old - m_new)
            p = jnp.exp2(s_ref[slot, hd] - m_new).astype(BF16)
            m_ref[hd] = m_new
            v_aug = jnp.concatenate(
                [vt_ref[0, hd * V_HEAD:(hd + 1) * V_HEAD, pl.ds(k_off, TKV)], ones], axis=0)
            acc_ref[hd] = alpha * acc_ref[hd] + jnp.dot(v_aug, p, preferred_element_type=F32)

    @pl.when(t == 0)
    def _():
        scores(0, 0, 0)

    m_ref[...] = jnp.full(m_ref.shape, -jnp.inf, F32)
    acc_ref[...] = jnp.zeros(acc_ref.shape, F32)

    for c in range(n_chunks):
        if c + 1 < n_chunks:
            scores(t, c + 1, (c + 1) % 2)
        else:
            scores(jnp.minimum(t + 1, n_tiles - 1), 0, (c + 1) % 2)
        consume(c, c % 2)
    out =[acc_ref[hd, :V_HEAD, :] * (1.0 / acc_ref[hd, V_HEAD:V_HEAD + 1, :])
           for hd in range(HEADS_PER_STEP)]
    o_ref[0] = jnp.concatenate(out, axis=0).T.astype(BF16)


def _attention(qt, k, vt):
    b, s, _ = k.shape
    width = HEADS_PER_STEP * HEAD_PAD
    vrows = HEADS_PER_STEP * V_HEAD
    return pl.pallas_call(
        _attn_kernel,
        grid=(b, MLA_HEADS // HEADS_PER_STEP, s // TQ),
        in_specs=[pl.BlockSpec((1, width, s), lambda i, j, t: (i, j, 0)),
                  pl.BlockSpec((1, s, width), lambda i, j, t: (i, 0, j)),
                  pl.BlockSpec((1, vrows, s), lambda i, j, t: (i, j, 0))],
        out_specs=pl.BlockSpec((1, TQ, vrows), lambda i, j, t: (i, t, j)),
        out_shape=jax.ShapeDtypeStruct((b, s, MAIN_WIDTH), BF16),
        scratch_shapes=[pltpu.VMEM((2, HEADS_PER_STEP, TKV, TQ), F32),
                        pltpu.VMEM((2, HEADS_PER_STEP, 1, TQ), F32),
                        pltpu.VMEM((HEADS_PER_STEP, 1, TQ), F32),
                        pltpu.VMEM((HEADS_PER_STEP, V_HEAD + SUM_ROWS, TQ), F32)],
        compiler_params=_params(3),
        name="mla_attention",
    )(qt, k, vt)


def _post_kernel(*refs, final_norm):
    if final_norm:
        x_ref, main_ref, qm_ref, gate_ref, kbd_ref, vbd_ref, wout_ref, fg_ref, o_ref = refs
    else:
        x_ref, main_ref, qm_ref, gate_ref, kbd_ref, vbd_ref, wout_ref, o_ref = refs
    sub = x_ref.shape[1] // POST_SUBTILES
    rows = [slice(p * sub, (p + 1) * sub) for p in range(POST_SUBTILES)]

    def scores(p, _):
        return jnp.dot(qm_ref[0, rows[p]], kbd_ref[0], preferred_element_type=F32)

    def softmax(p, s):
        probs = []
        for hd in range(MEM_HEADS):
            sh = s[:, hd * N_MEM:(hd + 1) * N_MEM]
            e = jnp.exp(sh - jnp.max(sh, axis=-1, keepdims=True))
            probs.append((e / jnp.sum(e, axis=-1, keepdims=True)).astype(BF16))
        return jnp.concatenate(probs, axis=-1)

    def values(p, probs):
        return jnp.dot(probs, vbd_ref[0], preferred_element_type=F32)

    def gating(p, mem_out):
        gate = gate_ref[0, rows[p]].astype(F32)
        branch = jnp.concatenate([main_ref[0, rows[p]].astype(F32), mem_out], axis=-1)
        return (branch * (gate * jax.nn.sigmoid(gate))).astype(BF16)

    def project(p, branch):
        y = x_ref[0, rows[p]] + jnp.dot(branch, wout_ref[...], preferred_element_type=F32)
        if final_norm:
            y = _rmsnorm(y, fg_ref[...])
        o_ref[0, rows[p]] = y

    stages = (scores, softmax, values, gating, project)
    state = [None] * POST_SUBTILES
    for step in range(len(stages) + POST_SUBTILES - 1):
        for p in range(POST_SUBTILES):
            if 0 <= step - p < len(stages):
                state[p] = stages[step - p](p, state[p])


def _post(x, main, qm, gate, kbd, vbd, w_out, final_g=None):
    b, s, _ = x.shape
    row = lambda i, j: (i, j, 0)
    per_batch = lambda i, j: (i, 0, 0)
    const = lambda i, j: (0, 0)
    in_specs = [pl.BlockSpec((1, TM_POST, D_MODEL), row),
                pl.BlockSpec((1, TM_POST, MAIN_WIDTH), row),
                pl.BlockSpec((1, TM_POST, MEM_WIDTH), row),
                pl.BlockSpec((1, TM_POST, MIX_WIDTH), row),
                pl.BlockSpec((1, MEM_WIDTH, MEM_HEADS * N_MEM), per_batch),
                pl.BlockSpec((1, MEM_HEADS * N_MEM, MEM_WIDTH), per_batch),
                pl.BlockSpec((MIX_WIDTH, D_MODEL), const)]
    args = [x, main, qm, gate, kbd, vbd, w_out.astype(BF16)]
    if final_g is not None:
        in_specs.append(pl.BlockSpec((1, D_MODEL), const))
        args.append(final_g.reshape(1, -1))
    return pl.pallas_call(
        functools.partial(_post_kernel, final_norm=final_g is not None),
        grid=(b, s // TM_POST),
        in_specs=in_specs,
        out_specs=pl.BlockSpec((1, TM_POST, D_MODEL), row),
        out_shape=jax.ShapeDtypeStruct((b, s, D_MODEL), F32),
        compiler_params=_params(2),
        name="post_final" if final_g is not None else "post",
    )(*args)


def _fold_kernel(wf_ref, wn_ref, cc_ref, sc_ref, oa_ref, ob_ref):
    hp = lax.Precision.HIGHEST
    a = jnp.dot(cc_ref[...], wn_ref[0], precision=hp, preferred_element_type=F32)
    bm = jnp.dot(sc_ref[...], wn_ref[0], precision=hp, preferred_element_type=F32)
    oa_ref[0] = jnp.dot(wf_ref[0], a, precision=hp, preferred_element_type=F32)
    ob_ref[0] = jnp.dot(wf_ref[0], bm, precision=hp, preferred_element_type=F32)


def _channel_dft_tables(seq_len):
    n = FNET_GROUP_DIM
    jk = np.outer(np.arange(n), np.arange(n)) % n
    ang = 2.0 * np.pi * jk / n
    alpha = 1.0 / np.sqrt(float(seq_len) * n)
    cc = np.zeros((GROUP_PAD, GROUP_PAD), np.float32)
    sc = np.zeros((GROUP_PAD, GROUP_PAD), np.float32)
    cc[:n, :n] = alpha * np.cos(ang)
    sc[:n, :n] = -alpha * np.sin(ang)
    return jnp.asarray(cc), jnp.asarray(sc)


def _fold_fnet(w_in_f, w_fnet, seq_len):
    gpad = GROUP_PAD - FNET_GROUP_DIM
    wf = jnp.pad(w_in_f.reshape(D_MODEL, FNET_GROUPS, FNET_GROUP_DIM).transpose(1, 0, 2),
                 ((0, 0), (0, 0), (0, gpad)))
    wn = jnp.pad(w_fnet, ((0, 0), (0, gpad), (0, gpad)))
    cc, sc = _channel_dft_tables(seq_len)
    grp = lambda g: (g, 0, 0)
    const = lambda g: (0, 0)
    oa, ob = pl.pallas_call(
        _fold_kernel,
        grid=(FNET_GROUPS,),
        in_specs=[pl.BlockSpec((1, D_MODEL, GROUP_PAD), grp),
                  pl.BlockSpec((1, GROUP_PAD, GROUP_PAD), grp),
                  pl.BlockSpec((GROUP_PAD, GROUP_PAD), const),
                  pl.BlockSpec((GROUP_PAD, GROUP_PAD), const)],
        out_specs=[pl.BlockSpec((1, D_MODEL, GROUP_PAD), grp)] * 2,
        out_shape=[jax.ShapeDtypeStruct((FNET_GROUPS, D_MODEL, GROUP_PAD), F32)] * 2,
        compiler_params=_params(1),
        name="fnet_fold",
    )(wf, wn, cc, sc)
    unpack = lambda o: o[:, :, :FNET_GROUP_DIM].transpose(1, 0, 2).reshape(D_MODEL, MAIN_WIDTH)
    return unpack(oa), unpack(ob)


_L1_IN = 2 * MAIN_WIDTH + MEM_WIDTH + MIX_WIDTH


def _l1_pre_kernel(x_ref, g_ref, w_ref, u_ref, v_ref, qm_ref, gate_ref):
    h = _rmsnorm(x_ref[0], g_ref[...]).astype(BF16)
    proj = jnp.dot(h, w_ref[...], preferred_element_type=F32)
    u_ref[0] = proj[:, :MAIN_WIDTH].astype(BF16)
    v_ref[0] = proj[:, MAIN_WIDTH:2 * MAIN_WIDTH].astype(BF16)
    qm_ref[0] = proj[:, 2 * MAIN_WIDTH:2 * MAIN_WIDTH + MEM_WIDTH].astype(BF16)
    gate_ref[0] = proj[:, 2 * MAIN_WIDTH + MEM_WIDTH:].astype(BF16)


def _l1_pre(x, norm_g, w_all):
    b, s, _ = x.shape
    row = lambda i, j: (i, j, 0)
    const = lambda i, j: (0, 0)
    return pl.pallas_call(
        _l1_pre_kernel,
        grid=(b, s // TM),
        in_specs=[pl.BlockSpec((1, TM, D_MODEL), row),
                  pl.BlockSpec((1, D_MODEL), const),
                  pl.BlockSpec((D_MODEL, _L1_IN), const)],
        out_specs=[pl.BlockSpec((1, TM, MAIN_WIDTH), row),
                   pl.BlockSpec((1, TM, MAIN_WIDTH), row),
                   pl.BlockSpec((1, TM, MEM_WIDTH), row),
                   pl.BlockSpec((1, TM, MIX_WIDTH), row)],
        out_shape=[jax.ShapeDtypeStruct((b, s, MAIN_WIDTH), BF16),
                   jax.ShapeDtypeStruct((b, s, MAIN_WIDTH), BF16),
                   jax.ShapeDtypeStruct((b, s, MEM_WIDTH), BF16),
                   jax.ShapeDtypeStruct((b, s, MIX_WIDTH), BF16)],
        compiler_params=_params(2),
        name="l1_pre",
    )(x, norm_g.reshape(1, -1), w_all)


def _dft3_tables():
    r = FFT_R
    i = np.arange(r)
    w = lambda m, e: np.exp(-2j * np.pi * (e % m) / m)
    f16 = w(r, np.outer(i, i))
    eye = np.eye(r)
    ga = np.einsum("ed,ka,bk->bekad", eye, f16, w(r * r, np.outer(i, i))).reshape(r, FFT_G, FFT_G)
    tw = w(r ** 3, i[:, None, None] * (i[None, None, :] + r * i[None, :, None]))
    gb = np.einsum("jk,qb,dqj->dqjbk", eye, f16, tw).reshape(r, FFT_G, FFT_G)
    gc = np.einsum("jk,qd->qjdk", eye, f16).reshape(FFT_G, FFT_G)
    stack = lambda g: np.concatenate([np.concatenate([g.real, -g.imag], axis=-1),
                                      np.concatenate([g.imag, g.real], axis=-1)], axis=-2)
    real_rows = np.concatenate([gc.real, -gc.imag], axis=-1)
    f32 = lambda a: jnp.asarray(a.astype(np.float32))
    return f32(stack(ga)), f32(stack(gb)), f32(real_rows)


def _dft3_kernel(u_ref, v_ref, ga_ref, gb_ref, gc_ref, o_ref, yr_ref, yi_ref):
    r, g = FFT_R, FFT_G
    tiles = (r, r, FFT_CW)

    def apply(mat, zr, zi):
        z = jnp.concatenate([zr.reshape(g, FFT_CW), zi.reshape(g, FFT_CW)], axis=0)
        return jnp.dot(mat, z, preferred_element_type=F32)

    def stage_a(b, carry):
        y = apply(ga_ref[b], u_ref[0, :, b], v_ref[0, :, b])
        yr_ref[:, b] = y[:g].astype(BF16).reshape(tiles)
        yi_ref[:, b] = y[g:].astype(BF16).reshape(tiles)
        return carry

    def stage_b(d, carry):
        y = apply(gb_ref[d], yr_ref[d], yi_ref[d])
        yr_ref[d] = y[:g].astype(BF16).reshape(tiles)
        yi_ref[d] = y[g:].astype(BF16).reshape(tiles)
        return carry

    def stage_c(kb, carry):
        y = apply(gc_ref[...], yr_ref[:, kb], yi_ref[:, kb])
        o_ref[0, :, kb] = y.astype(BF16).reshape(tiles)
        return carry

    lax.fori_loop(0, r, stage_a, 0, unroll=FFT_UNROLL)
    lax.fori_loop(0, r, stage_b, 0, unroll=FFT_UNROLL)
    lax.fori_loop(0, r, stage_c, 0, unroll=FFT_UNROLL)


def _seq_dft_real(u, v):
    b, s, w = u.shape
    r = FFT_R
    assert s == r ** 3 and w % FFT_CW == 0
    ga, gb, gc = (t.astype(BF16) for t in _dft3_tables())
    shape5 = (b, r, r, r, w)
    blk = pl.BlockSpec((1, r, r, r, FFT_CW), lambda i, c: (i, 0, 0, 0, c))
    out = pl.pallas_call(
        _dft3_kernel,
        grid=(b, w // FFT_CW),
        in_specs=[blk, blk,
                  pl.BlockSpec((r, 2 * FFT_G, 2 * FFT_G), lambda i, c: (0, 0, 0), pipeline_mode=pl.Buffered(1)),
                  pl.BlockSpec((r, 2 * FFT_G, 2 * FFT_G), lambda i, c: (0, 0, 0), pipeline_mode=pl.Buffered(1)),
                  pl.BlockSpec((FFT_G, 2 * FFT_G), lambda i, c: (0, 0), pipeline_mode=pl.Buffered(1))],
        out_specs=blk,
        out_shape=jax.ShapeDtypeStruct(shape5, BF16),
        scratch_shapes=[pltpu.VMEM((r, r, r, FFT_CW), BF16), pltpu.VMEM((r, r, r, FFT_CW), BF16)],
        compiler_params=_params(2),
        name="seq_dft",
    )(u.reshape(shape5), v.reshape(shape5), ga, gb, gc)
    return out.reshape(b, s, w)


def kernel(x, mem, positions, norm_g_l0, w_in_l0, q_norm_g_l0, kv_norm_g_l0, w_uq_l0, w_ukv_l0,
           mem_norm_g_l0, w_mem_kv_l0, w_out_l0, norm_g_l1, w_in_l1, w_fnet_l1, mem_norm_g_l1,
           w_mem_kv_l1, w_out_l1, final_norm_g):
    s = x.shape[1]
    cos_t, sin_t = _rope_tables(positions)
    kbd0, vbd0 = _mem_kv(mem, mem_norm_g_l0, w_mem_kv_l0)
    kbd1, vbd1 = _mem_kv(mem, mem_norm_g_l1, w_mem_kv_l1)

    qt, k, vt, qm0, gate0 = _l0_pre(x, cos_t, sin_t, norm_g_l0, w_in_l0, q_norm_g_l0, kv_norm_g_l0,
                                    w_uq_l0, w_ukv_l0)
    attn = _attention(qt, k, vt)
    x1 = _post(x, attn, qm0, gate0, kbd0, vbd0, w_out_l0)

    w_a, w_b = _fold_fnet(w_in_l1[:, :MAIN_WIDTH], w_fnet_l1, s)
    w_all = jnp.concatenate([w_a, w_b, w_in_l1[:, MAIN_WIDTH:]], axis=1).astype(BF16)
    u, v, qm1, gate1 = _l1_pre(x1, norm_g_l1, w_all)
    mixed = _seq_dft_real(u, v)
    return _post(x1, mixed, qm1, gate1, kbd1, vbd1, w_out_l1, final_g=final_norm_g)
```

```python
import functools

import numpy as np
import jax
import jax.numpy as jnp
from jax import lax
from jax.experimental import pallas as pl
from jax.experimental.pallas import tpu as pltpu

F32 = jnp.float32
BF16 = jnp.bfloat16

D_MODEL = 1024
N_MEM = 256
MIX_WIDTH = D_MODEL
MEM_HEADS = 4
MEM_WIDTH = MIX_WIDTH // 4
MEM_HEAD_DIM = MEM_WIDTH // MEM_HEADS
MAIN_WIDTH = MIX_WIDTH - MEM_WIDTH
MLA_HEADS = 12
QK_NOPE = 64
QK_ROPE = 32
V_HEAD = MAIN_WIDTH // MLA_HEADS
Q_LORA = (3 * D_MODEL) // 8
KV_LORA = D_MODEL // 4
ROPE_THETA = 10000.0
FNET_GROUPS = 4
FNET_GROUP_DIM = MAIN_WIDTH // FNET_GROUPS
EPS = 1e-6

LANES = 128
HEAD_PAD = LANES
QK_PAD = MLA_HEADS * HEAD_PAD
ROPE_LO = QK_NOPE
ROPE_HALF = QK_ROPE // 2
GROUP_PAD = 256

FFT_R = 16
FFT_G = FFT_R * FFT_R
FFT_CW = 256
FFT_UNROLL = 8

TM = 512
TQ = 512
TKV = 256
TM_L1 = 1024
TM_POST = 1024
POST_SUBTILES = 4

VMEM_LIMIT = 56 * 1024 * 1024


def _rmsnorm(x, g):
    return x * lax.rsqrt(jnp.mean(x * x, axis=-1, keepdims=True) + EPS) * g


def _params(n_grid_dims):
    return pltpu.CompilerParams(
        dimension_semantics=("arbitrary",) * n_grid_dims,
        vmem_limit_bytes=VMEM_LIMIT,
    )


def _rope_table_kernel(pos_ref, invf_ref, cos_ref, sin_ref):
    ang = invf_ref[...] * pos_ref[0].astype(F32)
    c, s = jnp.cos(ang), jnp.sin(ang)
    cos_ref[0] = jnp.concatenate([c, c], axis=0)
    sin_ref[0] = jnp.concatenate([-s, s], axis=0)


def _rope_tables(positions):
    b, s = positions.shape
    inv_freq = 1.0 / (ROPE_THETA ** (jnp.arange(0, QK_ROPE, 2, dtype=F32) / QK_ROPE))
    invf = inv_freq.reshape(ROPE_HALF, 1)
    return pl.pallas_call(
        _rope_table_kernel,
        grid=(b,),
        in_specs=[pl.BlockSpec((1, 1, s), lambda i: (i, 0, 0)),
                  pl.BlockSpec((ROPE_HALF, 1), lambda i: (0, 0))],
        out_specs=[pl.BlockSpec((1, QK_ROPE, s), lambda i: (i, 0, 0))] * 2,
        out_shape=[jax.ShapeDtypeStruct((b, QK_ROPE, s), F32)] * 2,
        compiler_params=_params(1),
        name="rope_tables",
    )(positions.reshape(b, 1, s), invf)


def _mem_kv_kernel(mem_ref, g_ref, w_ref, kbd_ref, vbd_ref):
    mn = _rmsnorm(mem_ref[0], g_ref[0]).astype(BF16)
    kv = jnp.dot(mn, w_ref[0], preferred_element_type=F32)
    k_t = (kv[:, :MEM_WIDTH] * (1.0 / float(np.sqrt(MEM_HEAD_DIM)))).T
    v = kv[:, MEM_WIDTH:]
    k_rep = jnp.concatenate([k_t] * MEM_HEADS, axis=1)
    r = lax.broadcasted_iota(jnp.int32, k_rep.shape, 0) // MEM_HEAD_DIM
    c = lax.broadcasted_iota(jnp.int32, k_rep.shape, 1) // N_MEM
    kbd_ref[0, 0] = jnp.where(r == c, k_rep, 0.0).astype(BF16)
    v_rep = jnp.concatenate([v] * MEM_HEADS, axis=0)
    r = lax.broadcasted_iota(jnp.int32, v_rep.shape, 0) // N_MEM
    c = lax.broadcasted_iota(jnp.int32, v_rep.shape, 1) // MEM_HEAD_DIM
    vbd_ref[0, 0] = jnp.where(r == c, v_rep, 0.0).astype(BF16)


def _mem_kv(mem, gains, weights):
    b = mem.shape[0]
    n_layers = gains.shape[0]
    per_layer = lambda i, l: (l, 0, 0)
    out = lambda i, l: (l, i, 0, 0)
    return pl.pallas_call(
        _mem_kv_kernel,
        grid=(b, n_layers),
        in_specs=[pl.BlockSpec((1, N_MEM, D_MODEL), lambda i, l: (i, 0, 0)),
                  pl.BlockSpec((1, 1, D_MODEL), per_layer),
                  pl.BlockSpec((1, D_MODEL, 2 * MEM_WIDTH), per_layer)],
        out_specs=[pl.BlockSpec((1, 1, MEM_WIDTH, MEM_HEADS * N_MEM), out),
                   pl.BlockSpec((1, 1, MEM_HEADS * N_MEM, MEM_WIDTH), out)],
        out_shape=[jax.ShapeDtypeStruct((n_layers, b, MEM_WIDTH, MEM_HEADS * N_MEM), BF16),
                   jax.ShapeDtypeStruct((n_layers, b, MEM_HEADS * N_MEM, MEM_WIDTH), BF16)],
        compiler_params=_params(2),
        name="mem_kv",
    )(mem, gains.reshape(n_layers, 1, D_MODEL), weights.astype(BF16))


_L0_CQ = (0, Q_LORA)
_L0_CKV = (_L0_CQ[1], _L0_CQ[1] + KV_LORA)
_L0_QM = (_L0_CKV[1], _L0_CKV[1] + MEM_WIDTH)
_L0_GATE = (_L0_QM[1], _L0_QM[1] + MIX_WIDTH)
_L0_KR = (_L0_GATE[1], _L0_GATE[1] + HEAD_PAD)
_L0_IN = _L0_KR[1]


def _l0_pre_kernel(x_ref, cos_ref, sin_ref, g_ref, w_in_ref, gq_ref, gkv_ref, wqt_ref, wk_ref,
                   wvt_ref, qt_ref, k_ref, vt_ref, qm_ref, gate_ref):
    h = _rmsnorm(x_ref[0], g_ref[...]).astype(BF16)
    proj = jnp.dot(h, w_in_ref[...], preferred_element_type=F32)
    qm_ref[0] = proj[:, _L0_QM[0]:_L0_QM[1]].astype(BF16)
    gate_ref[0] = proj[:, _L0_GATE[0]:_L0_GATE[1]].astype(BF16)
    cqn = _rmsnorm(proj[:, _L0_CQ[0]:_L0_CQ[1]], gq_ref[...]).astype(BF16)
    ckvn = _rmsnorm(proj[:, _L0_CKV[0]:_L0_CKV[1]], gkv_ref[...]).astype(BF16)
    nt = (((1,), (1,)), ((), ()))
    qt = lax.dot_general(wqt_ref[...], cqn, nt, preferred_element_type=F32)
    kn = jnp.dot(ckvn, wk_ref[...], preferred_element_type=F32)
    vt = lax.dot_general(wvt_ref[...], ckvn, nt, preferred_element_type=F32)
    vt_ref[0] = vt.astype(BF16)

    cos_t = cos_ref[0]
    sin_t = sin_ref[0]

    def rope_t(blk):
        partner = jnp.concatenate([blk[ROPE_HALF:], blk[:ROPE_HALF]], axis=0)
        return blk * cos_t + partner * sin_t

    rope_rows = slice(ROPE_LO, ROPE_LO + QK_ROPE)
    kr_t = proj[:, _L0_KR[0]:_L0_KR[1]].T
    k_rope = jnp.concatenate([kr_t[:ROPE_LO], rope_t(kr_t[rope_rows]), kr_t[ROPE_LO + QK_ROPE:]],
                             axis=0).T
    scale = float(np.log2(np.e)) / float(np.sqrt(QK_NOPE + QK_ROPE))
    for hd in range(MLA_HEADS):
        sl = slice(hd * HEAD_PAD, (hd + 1) * HEAD_PAD)
        q_h = qt[sl]
        q_h = jnp.concatenate([q_h[:ROPE_LO], rope_t(q_h[rope_rows]), q_h[ROPE_LO + QK_ROPE:]], axis=0)
        qt_ref[0, sl, :] = (q_h * scale).astype(BF16)
        k_ref[0, :, sl] = (kn[:, sl] + k_rope).astype(BF16)


def _l0_pre(x, cos_t, sin_t, norm_g, w_in, q_norm_g, kv_norm_g, w_uq, w_ukv):
    b, s, _ = x.shape
    o1, o2, o3, o4 = Q_LORA, Q_LORA + KV_LORA, Q_LORA + KV_LORA + QK_ROPE, Q_LORA + KV_LORA + QK_ROPE + MEM_WIDTH
    w_kr = jnp.pad(w_in[:, o2:o3], ((0, 0), (ROPE_LO, HEAD_PAD - ROPE_LO - QK_ROPE)))
    w_in_p = jnp.concatenate([w_in[:, :o2], w_in[:, o3:o4], w_in[:, o4:], w_kr], axis=1).astype(BF16)
    wqt = jnp.pad(w_uq.reshape(Q_LORA, MLA_HEADS, QK_NOPE + QK_ROPE),
                  ((0, 0), (0, 0), (0, HEAD_PAD - QK_NOPE - QK_ROPE))).reshape(Q_LORA, QK_PAD).T.astype(BF16)
    w_ukv3 = w_ukv.reshape(KV_LORA, MLA_HEADS, QK_NOPE + V_HEAD)
    wk = jnp.pad(w_ukv3[:, :, :QK_NOPE], ((0, 0), (0, 0), (0, HEAD_PAD - QK_NOPE))
                 ).reshape(KV_LORA, QK_PAD).astype(BF16)
    wvt = w_ukv3[:, :, QK_NOPE:].reshape(KV_LORA, MAIN_WIDTH).T.astype(BF16)

    row = lambda i, j: (i, j, 0)
    col = lambda i, j: (i, 0, j)
    const = lambda i, j: (0, 0)
    return pl.pallas_call(
        _l0_pre_kernel,
        grid=(b, s // TM),
        in_specs=[pl.BlockSpec((1, TM, D_MODEL), row),
                  pl.BlockSpec((1, QK_ROPE, TM), col),
                  pl.BlockSpec((1, QK_ROPE, TM), col),
                  pl.BlockSpec((1, D_MODEL), const),
                  pl.BlockSpec((D_MODEL, _L0_IN), const),
                  pl.BlockSpec((1, Q_LORA), const),
                  pl.BlockSpec((1, KV_LORA), const),
                  pl.BlockSpec((QK_PAD, Q_LORA), const),
                  pl.BlockSpec((KV_LORA, QK_PAD), const),
                  pl.BlockSpec((MAIN_WIDTH, KV_LORA), const)],
        out_specs=[pl.BlockSpec((1, QK_PAD, TM), col),
                   pl.BlockSpec((1, TM, QK_PAD), row),
                   pl.BlockSpec((1, MAIN_WIDTH, TM), col),
                   pl.BlockSpec((1, TM, MEM_WIDTH), row),
                   pl.BlockSpec((1, TM, MIX_WIDTH), row)],
        out_shape=[jax.ShapeDtypeStruct((b, QK_PAD, s), BF16),
                   jax.ShapeDtypeStruct((b, s, QK_PAD), BF16),
                   jax.ShapeDtypeStruct((b, MAIN_WIDTH, s), BF16),
                   jax.ShapeDtypeStruct((b, s, MEM_WIDTH), BF16),
                   jax.ShapeDtypeStruct((b, s, MIX_WIDTH), BF16)],
        compiler_params=_params(2),
        name="l0_pre",
    )(x, cos_t, sin_t, norm_g.reshape(1, -1), w_in_p, q_norm_g.reshape(1, -1),
      kv_norm_g.reshape(1, -1), wqt, wk, wvt)


HEADS_PER_STEP = 2
SUM_ROWS = 16


def _attn_kernel(qt_ref, k_ref, vt_ref, o_ref, s_ref, mx_ref, m_ref, acc_ref):
    n_tiles = k_ref.shape[1] // TQ
    n_chunks = k_ref.shape[1] // TKV
    assert n_chunks % 2 == 0

    def scores(tile, chunk, slot):
        q_off = pl.multiple_of(tile * TQ, TQ)
        k_off = pl.multiple_of(chunk * TKV, TKV)
        for hd in range(HEADS_PER_STEP):
            lanes = slice(hd * HEAD_PAD, (hd + 1) * HEAD_PAD)
            s_t = jnp.dot(k_ref[0, pl.ds(k_off, TKV), lanes], qt_ref[0, lanes, pl.ds(q_off, TQ)],
                          preferred_element_type=F32)
            s_ref[slot, hd] = s_t
            mx_ref[slot, hd] = jnp.max(s_t, axis=0, keepdims=True)

    def consume(chunk, slot):
        k_off = pl.multiple_of(chunk * TKV, TKV)
        ones = jnp.ones((SUM_ROWS, TKV), BF16)
        for hd in range(HEADS_PER_STEP):
            m_old = m_ref[hd]
            m_new = jnp.maximum(m_old, mx_ref[slot, hd])
            alpha = jnp.exp2(m_old - m_new)
            p = jnp.exp2(s_ref[slot, hd] - m_new).astype(BF16)
            m_ref[hd] = m_new
            v_aug = jnp.concatenate(
                [vt_ref[0, hd * V_HEAD:(hd + 1) * V_HEAD, pl.ds(k_off, TKV)], ones], axis=0)
            acc_ref[hd] = alpha * acc_ref[hd] + jnp.dot(v_aug, p, preferred_element_type=F32)

    def tile(t, carry):
        m_ref[...] = jnp.full(m_ref.shape, -jnp.inf, F32)
        acc_ref[...] = jnp.zeros(acc_ref.shape, F32)
        for c in range(n_chunks):
            if c + 1 < n_chunks:
                scores(t, c + 1, (c + 1) % 2)
            else:
                scores(jnp.minimum(t + 1, n_tiles - 1), 0, (c + 1) % 2)
            consume(c, c % 2)
        out = [acc_ref[hd, :V_HEAD, :] * (1.0 / acc_ref[hd, V_HEAD:V_HEAD + 1, :])
               for hd in range(HEADS_PER_STEP)]
        o_ref[0, pl.ds(pl.multiple_of(t * TQ, TQ), TQ), :] = jnp.concatenate(out, axis=0).T.astype(BF16)
        return carry

    scores(0, 0, 0)
    lax.fori_loop(0, n_tiles, tile, 0)


def _attention(qt, k, vt):
    b, s, _ = k.shape
    width = HEADS_PER_STEP * HEAD_PAD
    vrows = HEADS_PER_STEP * V_HEAD
    return pl.pallas_call(
        _attn_kernel,
        grid=(b, MLA_HEADS // HEADS_PER_STEP),
        in_specs=[pl.BlockSpec((1, width, s), lambda i, j: (i, j, 0)),
                  pl.BlockSpec((1, s, width), lambda i, j: (i, 0, j)),
                  pl.BlockSpec((1, vrows, s), lambda i, j: (i, j, 0))],
        out_specs=pl.BlockSpec((1, s, vrows), lambda i, j: (i, 0, j)),
        out_shape=jax.ShapeDtypeStruct((b, s, MAIN_WIDTH), BF16),
        scratch_shapes=[pltpu.VMEM((2, HEADS_PER_STEP, TKV, TQ), F32),
                        pltpu.VMEM((2, HEADS_PER_STEP, 1, TQ), F32),
                        pltpu.VMEM((HEADS_PER_STEP, 1, TQ), F32),
                        pltpu.VMEM((HEADS_PER_STEP, V_HEAD + SUM_ROWS, TQ), F32)],
        compiler_params=_params(2),
        name="mla_attention",
    )(qt, k, vt)


def _post_kernel(*refs, final_norm):
    if final_norm:
        x_ref, main_ref, qm_ref, gate_ref, kbd_ref, vbd_ref, wout_ref, fg_ref, o_ref = refs
    else:
        x_ref, main_ref, qm_ref, gate_ref, kbd_ref, vbd_ref, wout_ref, o_ref = refs
    sub = x_ref.shape[1] // POST_SUBTILES
    rows = [slice(p * sub, (p + 1) * sub) for p in range(POST_SUBTILES)]

    def scores(p, _):
        return jnp.dot(qm_ref[0, rows[p]], kbd_ref[0, 0], preferred_element_type=F32)

    def softmax(p, s):
        probs = []
        for hd in range(MEM_HEADS):
            sh = s[:, hd * N_MEM:(hd + 1) * N_MEM]
            e = jnp.exp(sh - jnp.max(sh, axis=-1, keepdims=True))
            probs.append((e / jnp.sum(e, axis=-1, keepdims=True)).astype(BF16))
        return jnp.concatenate(probs, axis=-1)

    def values(p, probs):
        return jnp.dot(probs, vbd_ref[0, 0], preferred_element_type=F32)

    def gating(p, mem_out):
        gate = gate_ref[0, rows[p]].astype(F32)
        branch = jnp.concatenate([main_ref[0, rows[p]].astype(F32), mem_out], axis=-1)
        return (branch * (gate * jax.nn.sigmoid(gate))).astype(BF16)

    def project(p, branch):
        y = x_ref[0, rows[p]] + jnp.dot(branch, wout_ref[...], preferred_element_type=F32)
        if final_norm:
            y = _rmsnorm(y, fg_ref[...])
        o_ref[0, rows[p]] = y

    stages = (scores, softmax, values, gating, project)
    state = [None] * POST_SUBTILES
    for step in range(len(stages) + POST_SUBTILES - 1):
        for p in range(POST_SUBTILES):
            if 0 <= step - p < len(stages):
                state[p] = stages[step - p](p, state[p])


def _post(x, main, qm, gate, kbd, vbd, layer, w_out, final_g=None):
    b, s, _ = x.shape
    row = lambda i, j: (i, j, 0)
    per_batch = lambda i, j: (layer, i, 0, 0)
    const = lambda i, j: (0, 0)
    in_specs = [pl.BlockSpec((1, TM_POST, D_MODEL), row),
                pl.BlockSpec((1, TM_POST, MAIN_WIDTH), row),
                pl.BlockSpec((1, TM_POST, MEM_WIDTH), row),
                pl.BlockSpec((1, TM_POST, MIX_WIDTH), row),
                pl.BlockSpec((1, 1, MEM_WIDTH, MEM_HEADS * N_MEM), per_batch),
                pl.BlockSpec((1, 1, MEM_HEADS * N_MEM, MEM_WIDTH), per_batch),
                pl.BlockSpec((MIX_WIDTH, D_MODEL), const)]
    args = [x, main, qm, gate, kbd, vbd, w_out.astype(BF16)]
    if final_g is not None:
        in_specs.append(pl.BlockSpec((1, D_MODEL), const))
        args.append(final_g.reshape(1, -1))
    return pl.pallas_call(
        functools.partial(_post_kernel, final_norm=final_g is not None),
        grid=(b, s // TM_POST),
        in_specs=in_specs,
        out_specs=pl.BlockSpec((1, TM_POST, D_MODEL), row),
        out_shape=jax.ShapeDtypeStruct((b, s, D_MODEL), F32),
        compiler_params=_params(2),
        name="post_final" if final_g is not None else "post",
    )(*args)


def _fold_kernel(wf_ref, wn_ref, cc_ref, sc_ref, oa_ref, ob_ref):
    hp = lax.Precision.HIGHEST
    a = jnp.dot(cc_ref[...], wn_ref[0], precision=hp, preferred_element_type=F32)
    bm = jnp.dot(sc_ref[...], wn_ref[0], precision=hp, preferred_element_type=F32)
    oa_ref[0] = jnp.dot(wf_ref[0], a, precision=hp, preferred_element_type=F32)
    ob_ref[0] = jnp.dot(wf_ref[0], bm, precision=hp, preferred_element_type=F32)


def _channel_dft_tables(seq_len):
    n = FNET_GROUP_DIM
    jk = np.outer(np.arange(n), np.arange(n)) % n
    ang = 2.0 * np.pi * jk / n
    alpha = 1.0 / np.sqrt(float(seq_len) * n)
    cc = np.zeros((GROUP_PAD, GROUP_PAD), np.float32)
    sc = np.zeros((GROUP_PAD, GROUP_PAD), np.float32)
    cc[:n, :n] = alpha * np.cos(ang)
    sc[:n, :n] = -alpha * np.sin(ang)
    return jnp.asarray(cc), jnp.asarray(sc)


def _fold_fnet(w_in_f, w_fnet, seq_len):
    gpad = GROUP_PAD - FNET_GROUP_DIM
    wf = jnp.pad(w_in_f.reshape(D_MODEL, FNET_GROUPS, FNET_GROUP_DIM).transpose(1, 0, 2),
                 ((0, 0), (0, 0), (0, gpad)))
    wn = jnp.pad(w_fnet, ((0, 0), (0, gpad), (0, gpad)))
    cc, sc = _channel_dft_tables(seq_len)
    grp = lambda g: (g, 0, 0)
    const = lambda g: (0, 0)
    oa, ob = pl.pallas_call(
        _fold_kernel,
        grid=(FNET_GROUPS,),
        in_specs=[pl.BlockSpec((1, D_MODEL, GROUP_PAD), grp),
                  pl.BlockSpec((1, GROUP_PAD, GROUP_PAD), grp),
                  pl.BlockSpec((GROUP_PAD, GROUP_PAD), const),
                  pl.BlockSpec((GROUP_PAD, GROUP_PAD), const)],
        out_specs=[pl.BlockSpec((1, D_MODEL, GROUP_PAD), grp)] * 2,
        out_shape=[jax.ShapeDtypeStruct((FNET_GROUPS, D_MODEL, GROUP_PAD), F32)] * 2,
        compiler_params=_params(1),
        name="fnet_fold",
    )(wf, wn, cc, sc)
    unpack = lambda o: o[:, :, :FNET_GROUP_DIM].transpose(1, 0, 2).reshape(D_MODEL, MAIN_WIDTH)
    return unpack(oa), unpack(ob)


_L1_IN = 2 * MAIN_WIDTH + MEM_WIDTH + MIX_WIDTH


def _l1_pre_kernel(x_ref, g_ref, w_ref, u_ref, v_ref, qm_ref, gate_ref):
    h = _rmsnorm(x_ref[0], g_ref[...]).astype(BF16)
    proj = jnp.dot(h, w_ref[...], preferred_element_type=F32)
    u_ref[0] = proj[:, :MAIN_WIDTH].astype(BF16)
    v_ref[0] = proj[:, MAIN_WIDTH:2 * MAIN_WIDTH].astype(BF16)
    qm_ref[0] = proj[:, 2 * MAIN_WIDTH:2 * MAIN_WIDTH + MEM_WIDTH].astype(BF16)
    gate_ref[0] = proj[:, 2 * MAIN_WIDTH + MEM_WIDTH:].astype(BF16)


def _l1_pre(x, norm_g, w_all):
    b, s, _ = x.shape
    row = lambda i, j: (i, j, 0)
    const = lambda i, j: (0, 0)
    return pl.pallas_call(
        _l1_pre_kernel,
        grid=(b, s // TM_L1),
        in_specs=[pl.BlockSpec((1, TM_L1, D_MODEL), row),
                  pl.BlockSpec((1, D_MODEL), const),
                  pl.BlockSpec((D_MODEL, _L1_IN), const, pipeline_mode=pl.Buffered(1))],
        out_specs=[pl.BlockSpec((1, TM_L1, MAIN_WIDTH), row),
                   pl.BlockSpec((1, TM_L1, MAIN_WIDTH), row),
                   pl.BlockSpec((1, TM_L1, MEM_WIDTH), row),
                   pl.BlockSpec((1, TM_L1, MIX_WIDTH), row)],
        out_shape=[jax.ShapeDtypeStruct((b, s, MAIN_WIDTH), BF16),
                   jax.ShapeDtypeStruct((b, s, MAIN_WIDTH), BF16),
                   jax.ShapeDtypeStruct((b, s, MEM_WIDTH), BF16),
                   jax.ShapeDtypeStruct((b, s, MIX_WIDTH), BF16)],
        compiler_params=_params(2),
        name="l1_pre",
    )(x, norm_g.reshape(1, -1), w_all)


def _dft3_tables():
    r = FFT_R
    i = np.arange(r)
    w = lambda m, e: np.exp(-2j * np.pi * (e % m) / m)
    f16 = w(r, np.outer(i, i))
    eye = np.eye(r)
    ga = np.einsum("ed,ka,bk->bekad", eye, f16, w(r * r, np.outer(i, i))).reshape(r, FFT_G, FFT_G)
    tw = w(r ** 3, i[:, None, None] * (i[None, None, :] + r * i[None, :, None]))
    gb = np.einsum("jk,qb,dqj->dqjbk", eye, f16, tw).reshape(r, FFT_G, FFT_G)
    gc = np.einsum("jk,qd->qjdk", eye, f16).reshape(FFT_G, FFT_G)
    stack = lambda g: np.concatenate([np.concatenate([g.real, -g.imag], axis=-1),
                                      np.concatenate([g.imag, g.real], axis=-1)], axis=-2)
    real_rows = np.concatenate([gc.real, -gc.imag], axis=-1)
    f32 = lambda a: jnp.asarray(a.astype(np.float32))
    return f32(stack(ga)), f32(stack(gb)), f32(real_rows)


def _dft3_kernel(u_ref, v_ref, ga_ref, gb_ref, gc_ref, o_ref, yr_ref, yi_ref):
    r, g = FFT_R, FFT_G
    tiles = (r, r, FFT_CW)

    def apply(mat, zr, zi):
        z = jnp.concatenate([zr.reshape(g, FFT_CW), zi.reshape(g, FFT_CW)], axis=0)
        return jnp.dot(mat, z, preferred_element_type=F32)

    def stage_a(b, carry):
        y = apply(ga_ref[b], u_ref[0, :, b], v_ref[0, :, b])
        yr_ref[:, b] = y[:g].astype(BF16).reshape(tiles)
        yi_ref[:, b] = y[g:].astype(BF16).reshape(tiles)
        return carry

    def stage_b(d, carry):
        y = apply(gb_ref[d], yr_ref[d], yi_ref[d])
        yr_ref[d] = y[:g].astype(BF16).reshape(tiles)
        yi_ref[d] = y[g:].astype(BF16).reshape(tiles)
        return carry

    def stage_c(kb, carry):
        y = apply(gc_ref[...], yr_ref[:, kb], yi_ref[:, kb])
        o_ref[0, :, kb] = y.astype(BF16).reshape(tiles)
        return carry

    lax.fori_loop(0, r, stage_a, 0, unroll=FFT_UNROLL)
    lax.fori_loop(0, r, stage_b, 0, unroll=FFT_UNROLL)
    lax.fori_loop(0, r, stage_c, 0, unroll=FFT_UNROLL)


def _seq_dft_real(u, v):
    b, s, w = u.shape
    r = FFT_R
    assert s == r ** 3 and w % FFT_CW == 0
    ga, gb, gc = (t.astype(BF16) for t in _dft3_tables())
    shape5 = (b, r, r, r, w)
    blk = pl.BlockSpec((1, r, r, r, FFT_CW), lambda i, c: (i, 0, 0, 0, c))
    out = pl.pallas_call(
        _dft3_kernel,
        grid=(b, w // FFT_CW),
        in_specs=[blk, blk,
                  pl.BlockSpec((r, 2 * FFT_G, 2 * FFT_G), lambda i, c: (0, 0, 0), pipeline_mode=pl.Buffered(1)),
                  pl.BlockSpec((r, 2 * FFT_G, 2 * FFT_G), lambda i, c: (0, 0, 0), pipeline_mode=pl.Buffered(1)),
                  pl.BlockSpec((FFT_G, 2 * FFT_G), lambda i, c: (0, 0), pipeline_mode=pl.Buffered(1))],
        out_specs=blk,
        out_shape=jax.ShapeDtypeStruct(shape5, BF16),
        scratch_shapes=[pltpu.VMEM((r, r, r, FFT_CW), BF16), pltpu.VMEM((r, r, r, FFT_CW), BF16)],
        compiler_params=_params(2),
        name="seq_dft",
    )(u.reshape(shape5), v.reshape(shape5), ga, gb, gc)
    return out.reshape(b, s, w)


def kernel(x, mem, positions, norm_g_l0, w_in_l0, q_norm_g_l0, kv_norm_g_l0, w_uq_l0, w_ukv_l0,
           mem_norm_g_l0, w_mem_kv_l0, w_out_l0, norm_g_l1, w_in_l1, w_fnet_l1, mem_norm_g_l1,
           w_mem_kv_l1, w_out_l1, final_norm_g):
    s = x.shape[1]
    cos_t, sin_t = _rope_tables(positions)
    kbd, vbd = _mem_kv(mem, jnp.stack([mem_norm_g_l0, mem_norm_g_l1]),
                       jnp.stack([w_mem_kv_l0, w_mem_kv_l1]))

    qt, k, vt, qm0, gate0 = _l0_pre(x, cos_t, sin_t, norm_g_l0, w_in_l0, q_norm_g_l0, kv_norm_g_l0,
                                    w_uq_l0, w_ukv_l0)
    attn = _attention(qt, k, vt)
    x1 = _post(x, attn, qm0, gate0, kbd, vbd, 0, w_out_l0)

    w_a, w_b = _fold_fnet(w_in_l1[:, :MAIN_WIDTH], w_fnet_l1, s)
    w_all = jnp.concatenate([w_a, w_b, w_in_l1[:, MAIN_WIDTH:]], axis=1).astype(BF16)
    u, v, qm1, gate1 = _l1_pre(x1, norm_g_l1, w_all)
    mixed = _seq_dft_real(u, v)
    return _post(x1, mixed, qm1, gate1, kbd, vbd, 1, w_out_l1, final_g=final_norm_g)
```

```python
import functools

import numpy as np
import jax
import jax.numpy as jnp
from jax import lax
from jax.experimental import pallas as pl
from jax.experimental.pallas import tpu as pltpu

F32 = jnp.float32
BF16 = jnp.bfloat16

D_MODEL = 1024
N_MEM = 256
MIX_WIDTH = D_MODEL
MEM_HEADS = 4
MEM_WIDTH = MIX_WIDTH // 4
MEM_HEAD_DIM = MEM_WIDTH // MEM_HEADS
MAIN_WIDTH = MIX_WIDTH - MEM_WIDTH
MLA_HEADS = 12
QK_NOPE = 64
QK_ROPE = 32
V_HEAD = MAIN_WIDTH // MLA_HEADS
Q_LORA = (3 * D_MODEL) // 8
KV_LORA = D_MODEL // 4
ROPE_THETA = 10000.0
FNET_GROUPS = 4
FNET_GROUP_DIM = MAIN_WIDTH // FNET_GROUPS
EPS = 1e-6

LANES = 128
HEAD_PAD = LANES
QK_PAD = MLA_HEADS * HEAD_PAD
QK_DIM = QK_NOPE + QK_ROPE
ROPE_LO = QK_NOPE
ROPE_HALF = QK_ROPE // 2
GROUP_PAD = 256

FFT_R = 16
FFT_G = FFT_R * FFT_R
FFT_CW = 256
FFT_UNROLL = 16

TM = 512
TQ = 512
TKV = 256
TM_L1 = 1024
TM_POST = 1024
POST_SUBTILES = 4

VMEM_LIMIT = 56 * 1024 * 1024


def _rmsnorm(x, g):
    return x * lax.rsqrt(jnp.mean(x * x, axis=-1, keepdims=True) + EPS) * g


def _params(n_grid_dims):
    return pltpu.CompilerParams(
        dimension_semantics=("arbitrary",) * n_grid_dims,
        vmem_limit_bytes=VMEM_LIMIT,
    )


def _rope_table_kernel(pos_ref, invf_ref, cos_ref, sin_ref):
    ang = invf_ref[...] * pos_ref[0].astype(F32)
    c, s = jnp.cos(ang), jnp.sin(ang)
    cos_ref[0] = jnp.concatenate([c, c], axis=0)
    sin_ref[0] = jnp.concatenate([-s, s], axis=0)


def _rope_tables(positions):
    b, s = positions.shape
    inv_freq = 1.0 / (ROPE_THETA ** (jnp.arange(0, QK_ROPE, 2, dtype=F32) / QK_ROPE))
    invf = inv_freq.reshape(ROPE_HALF, 1)
    return pl.pallas_call(
        _rope_table_kernel,
        grid=(b,),
        in_specs=[pl.BlockSpec((1, 1, s), lambda i: (i, 0, 0)),
                  pl.BlockSpec((ROPE_HALF, 1), lambda i: (0, 0))],
        out_specs=[pl.BlockSpec((1, QK_ROPE, s), lambda i: (i, 0, 0))] * 2,
        out_shape=[jax.ShapeDtypeStruct((b, QK_ROPE, s), F32)] * 2,
        compiler_params=_params(1),
        name="rope_tables",
    )(positions.reshape(b, 1, s), invf)


def _mem_kv_kernel(mem_ref, g_ref, w_ref, kbd_ref, vbd_ref):
    mn = _rmsnorm(mem_ref[0], g_ref[0]).astype(BF16)
    kv = jnp.dot(mn, w_ref[0], preferred_element_type=F32)
    k_t = (kv[:, :MEM_WIDTH] * (1.0 / float(np.sqrt(MEM_HEAD_DIM)))).T
    v = kv[:, MEM_WIDTH:]
    k_rep = jnp.concatenate([k_t] * MEM_HEADS, axis=1)
    r = lax.broadcasted_iota(jnp.int32, k_rep.shape, 0) // MEM_HEAD_DIM
    c = lax.broadcasted_iota(jnp.int32, k_rep.shape, 1) // N_MEM
    kbd_ref[0, 0] = jnp.where(r == c, k_rep, 0.0).astype(BF16)
    v_rep = jnp.concatenate([v] * MEM_HEADS, axis=0)
    r = lax.broadcasted_iota(jnp.int32, v_rep.shape, 0) // N_MEM
    c = lax.broadcasted_iota(jnp.int32, v_rep.shape, 1) // MEM_HEAD_DIM
    vbd_ref[0, 0] = jnp.where(r == c, v_rep, 0.0).astype(BF16)


def _mem_kv(mem, gains, weights):
    b = mem.shape[0]
    n_layers = gains.shape[0]
    per_layer = lambda i, l: (l, 0, 0)
    out = lambda i, l: (l, i, 0, 0)
    return pl.pallas_call(
        _mem_kv_kernel,
        grid=(b, n_layers),
        in_specs=[pl.BlockSpec((1, N_MEM, D_MODEL), lambda i, l: (i, 0, 0)),
                  pl.BlockSpec((1, 1, D_MODEL), per_layer),
                  pl.BlockSpec((1, D_MODEL, 2 * MEM_WIDTH), per_layer)],
        out_specs=[pl.BlockSpec((1, 1, MEM_WIDTH, MEM_HEADS * N_MEM), out),
                   pl.BlockSpec((1, 1, MEM_HEADS * N_MEM, MEM_WIDTH), out)],
        out_shape=[jax.ShapeDtypeStruct((n_layers, b, MEM_WIDTH, MEM_HEADS * N_MEM), BF16),
                   jax.ShapeDtypeStruct((n_layers, b, MEM_HEADS * N_MEM, MEM_WIDTH), BF16)],
        compiler_params=_params(2),
        name="mem_kv",
    )(mem, gains.reshape(n_layers, 1, D_MODEL), weights.astype(BF16))


_L0_CQ = (0, Q_LORA)
_L0_CKV = (_L0_CQ[1], _L0_CQ[1] + KV_LORA)
_L0_QM = (_L0_CKV[1], _L0_CKV[1] + MEM_WIDTH)
_L0_GATE = (_L0_QM[1], _L0_QM[1] + MIX_WIDTH)
_L0_KR = (_L0_GATE[1], _L0_GATE[1] + HEAD_PAD)
_L0_IN = _L0_KR[1]


def _l0_pre_kernel(x_ref, cos_ref, sin_ref, g_ref, w_in_ref, gq_ref, gkv_ref, wqt_ref, wk_ref,
                   wvt_ref, qt_ref, k_ref, vt_ref, qm_ref, gate_ref):
    h = _rmsnorm(x_ref[0], g_ref[...]).astype(BF16)
    proj = jnp.dot(h, w_in_ref[...], preferred_element_type=F32)
    qm_ref[0] = proj[:, _L0_QM[0]:_L0_QM[1]].astype(BF16)
    gate_ref[0] = proj[:, _L0_GATE[0]:_L0_GATE[1]].astype(BF16)
    cqn = _rmsnorm(proj[:, _L0_CQ[0]:_L0_CQ[1]], gq_ref[...]).astype(BF16)
    ckvn = _rmsnorm(proj[:, _L0_CKV[0]:_L0_CKV[1]], gkv_ref[...]).astype(BF16)
    nt = (((1,), (1,)), ((), ()))
    qt = lax.dot_general(wqt_ref[...], cqn, nt, preferred_element_type=F32)
    kn = jnp.dot(ckvn, wk_ref[...], preferred_element_type=F32)
    vt = lax.dot_general(wvt_ref[...], ckvn, nt, preferred_element_type=F32)
    vt_ref[0] = vt.astype(BF16)

    cos_t = cos_ref[0]
    sin_t = sin_ref[0]

    def rope_t(blk):
        partner = jnp.concatenate([blk[ROPE_HALF:], blk[:ROPE_HALF]], axis=0)
        return blk * cos_t + partner * sin_t

    rope_rows = slice(ROPE_LO, ROPE_LO + QK_ROPE)
    kr_t = proj[:, _L0_KR[0]:_L0_KR[1]].T
    k_rope = jnp.concatenate([kr_t[:ROPE_LO], rope_t(kr_t[rope_rows]), kr_t[ROPE_LO + QK_ROPE:]],
                             axis=0).T
    scale = float(np.log2(np.e)) / float(np.sqrt(QK_NOPE + QK_ROPE))
    nope_lane = lax.broadcasted_iota(jnp.int32, k_rope.shape, 1) < QK_NOPE
    for hd in range(MLA_HEADS):
        sl = slice(hd * HEAD_PAD, (hd + 1) * HEAD_PAD)
        rows = slice(hd * QK_DIM, (hd + 1) * QK_DIM)
        q_h = qt[rows]
        q_h = jnp.concatenate([q_h[:ROPE_LO], rope_t(q_h[rope_rows])], axis=0)
        qt_ref[0, rows, :] = (q_h * scale).astype(BF16)
        pair = kn[:, (hd // 2) * HEAD_PAD:(hd // 2 + 1) * HEAD_PAD]
        if hd % 2:
            pair = pltpu.roll(pair, QK_NOPE, 1)
        k_ref[0, :, sl] = jnp.where(nope_lane, pair, k_rope).astype(BF16)


def _l0_pre(x, cos_t, sin_t, norm_g, w_in, q_norm_g, kv_norm_g, w_uq, w_ukv):
    b, s, _ = x.shape
    o1, o2, o3, o4 = Q_LORA, Q_LORA + KV_LORA, Q_LORA + KV_LORA + QK_ROPE, Q_LORA + KV_LORA + QK_ROPE + MEM_WIDTH
    w_kr = jnp.pad(w_in[:, o2:o3], ((0, 0), (ROPE_LO, HEAD_PAD - ROPE_LO - QK_ROPE)))
    w_in_p = jnp.concatenate([w_in[:, :o2], w_in[:, o3:o4], w_in[:, o4:], w_kr], axis=1).astype(BF16)
    wqt = w_uq.T.astype(BF16)
    w_ukv3 = w_ukv.reshape(KV_LORA, MLA_HEADS, QK_NOPE + V_HEAD)
    wk = w_ukv3[:, :, :QK_NOPE].reshape(KV_LORA, MLA_HEADS * QK_NOPE).astype(BF16)
    wvt = w_ukv3[:, :, QK_NOPE:].reshape(KV_LORA, MAIN_WIDTH).T.astype(BF16)

    row = lambda i, j: (i, j, 0)
    col = lambda i, j: (i, 0, j)
    const = lambda i, j: (0, 0)
    return pl.pallas_call(
        _l0_pre_kernel,
        grid=(b, s // TM),
        in_specs=[pl.BlockSpec((1, TM, D_MODEL), row),
                  pl.BlockSpec((1, QK_ROPE, TM), col),
                  pl.BlockSpec((1, QK_ROPE, TM), col),
                  pl.BlockSpec((1, D_MODEL), const),
                  pl.BlockSpec((D_MODEL, _L0_IN), const),
                  pl.BlockSpec((1, Q_LORA), const),
                  pl.BlockSpec((1, KV_LORA), const),
                  pl.BlockSpec((MLA_HEADS * QK_DIM, Q_LORA), const),
                  pl.BlockSpec((KV_LORA, MLA_HEADS * QK_NOPE), const),
                  pl.BlockSpec((MAIN_WIDTH, KV_LORA), const)],
        out_specs=[pl.BlockSpec((1, MLA_HEADS * QK_DIM, TM), col),
                   pl.BlockSpec((1, TM, QK_PAD), row),
                   pl.BlockSpec((1, MAIN_WIDTH, TM), col),
                   pl.BlockSpec((1, TM, MEM_WIDTH), row),
                   pl.BlockSpec((1, TM, MIX_WIDTH), row)],
        out_shape=[jax.ShapeDtypeStruct((b, MLA_HEADS * QK_DIM, s), BF16),
                   jax.ShapeDtypeStruct((b, s, QK_PAD), BF16),
                   jax.ShapeDtypeStruct((b, MAIN_WIDTH, s), BF16),
                   jax.ShapeDtypeStruct((b, s, MEM_WIDTH), BF16),
                   jax.ShapeDtypeStruct((b, s, MIX_WIDTH), BF16)],
        compiler_params=_params(2),
        name="l0_pre",
    )(x, cos_t, sin_t, norm_g.reshape(1, -1), w_in_p, q_norm_g.reshape(1, -1),
      kv_norm_g.reshape(1, -1), wqt, wk, wvt)


HEADS_PER_STEP = 2
SUM_ROWS = 16


def _attn_kernel(qt_ref, k_ref, vt_ref, o_ref, s_ref, mx_ref, m_ref, acc_ref):
    n_tiles = k_ref.shape[1] // TQ
    n_chunks = k_ref.shape[1] // TKV
    assert n_chunks % 2 == 0

    def scores(hd, tile, chunk, slot):
        q_off = pl.multiple_of(tile * TQ, TQ)
        k_off = pl.multiple_of(chunk * TKV, TKV)
        lanes = slice(hd * HEAD_PAD, (hd + 1) * HEAD_PAD)
        q_t = jnp.concatenate([qt_ref[0, hd * QK_DIM:(hd + 1) * QK_DIM, pl.ds(q_off, TQ)],
                               jnp.zeros((HEAD_PAD - QK_DIM, TQ), BF16)], axis=0)
        s_t = jnp.dot(k_ref[0, pl.ds(k_off, TKV), lanes], q_t, preferred_element_type=F32)
        s_ref[slot, hd] = s_t
        mx_ref[slot, hd] = jnp.max(s_t, axis=0, keepdims=True)

    def consume(hd, chunk, slot):
        k_off = pl.multiple_of(chunk * TKV, TKV)
        m_old = m_ref[hd]
        m_new = jnp.maximum(m_old, mx_ref[slot, hd])
        alpha = jnp.exp2(m_old - m_new)
        p = jnp.exp2(s_ref[slot, hd] - m_new).astype(BF16)
        m_ref[hd] = m_new
        v_aug = jnp.concatenate([vt_ref[0, hd * V_HEAD:(hd + 1) * V_HEAD, pl.ds(k_off, TKV)],
                                 jnp.ones((SUM_ROWS, TKV), BF16)], axis=0)
        acc_ref[hd] = alpha * acc_ref[hd] + jnp.dot(v_aug, p, preferred_element_type=F32)

    def tile(t, carry):
        m_ref[...] = jnp.full(m_ref.shape, -jnp.inf, F32)
        acc_ref[...] = jnp.zeros(acc_ref.shape, F32)
        for c in range(n_chunks):
            nxt = (t, c + 1) if c + 1 < n_chunks else (jnp.minimum(t + 1, n_tiles - 1), 0)
            for hd in range(HEADS_PER_STEP):
                scores(hd, *nxt, (c + 1) % 2)
                consume(hd, c, c % 2)
        out =[acc_ref[hd, :V_HEAD, :] * (1.0 / acc_ref[hd, V_HEAD:V_HEAD + 1, :])
               for hd in range(HEADS_PER_STEP)]
        o_ref[0, pl.ds(pl.multiple_of(t * TQ, TQ), TQ), :] = jnp.concatenate(out, axis=0).T.astype(BF16)
        return carry

    for hd in range(HEADS_PER_STEP):
        scores(hd, 0, 0, 0)
    lax.fori_loop(0, n_tiles, tile, 0)


def _attention(qt, k, vt):
    b, s, _ = k.shape
    width = HEADS_PER_STEP * HEAD_PAD
    vrows = HEADS_PER_STEP * V_HEAD
    return pl.pallas_call(
        _attn_kernel,
        grid=(b, MLA_HEADS // HEADS_PER_STEP),
        in_specs=[pl.BlockSpec((1, HEADS_PER_STEP * QK_DIM, s), lambda i, j: (i, j, 0)),
                  pl.BlockSpec((1, s, width), lambda i, j: (i, 0, j)),
                  pl.BlockSpec((1, vrows, s), lambda i, j: (i, j, 0))],
        out_specs=pl.BlockSpec((1, s, vrows), lambda i, j: (i, 0, j)),
        out_shape=jax.ShapeDtypeStruct((b, s, MAIN_WIDTH), BF16),
        scratch_shapes=[pltpu.VMEM((2, HEADS_PER_STEP, TKV, TQ), F32),
                        pltpu.VMEM((2, HEADS_PER_STEP, 1, TQ), F32),
                        pltpu.VMEM((HEADS_PER_STEP, 1, TQ), F32),
                        pltpu.VMEM((HEADS_PER_STEP, V_HEAD + SUM_ROWS, TQ), F32)],
        compiler_params=_params(2),
        name="mla_attention",
    )(qt, k, vt)


def _post_kernel(*refs, final_norm):
    if final_norm:
        x_ref, main_ref, qm_ref, gate_ref, kbd_ref, vbd_ref, wout_ref, fg_ref, o_ref = refs
    else:
        x_ref, main_ref, qm_ref, gate_ref, kbd_ref, vbd_ref, wout_ref, o_ref = refs
    sub = x_ref.shape[1] // POST_SUBTILES
    rows = [slice(p * sub, (p + 1) * sub) for p in range(POST_SUBTILES)]

    def scores(p, _):
        return jnp.dot(qm_ref[0, rows[p]], kbd_ref[0, 0], preferred_element_type=F32)

    def softmax(p, s):
        probs = []
        for hd in range(MEM_HEADS):
            sh = s[:, hd * N_MEM:(hd + 1) * N_MEM]
            e = jnp.exp(sh - jnp.max(sh, axis=-1, keepdims=True))
            probs.append((e / jnp.sum(e, axis=-1, keepdims=True)).astype(BF16))
        return jnp.concatenate(probs, axis=-1)

    def values(p, probs):
        return jnp.dot(probs, vbd_ref[0, 0], preferred_element_type=F32)

    def gating(p, mem_out):
        gate = gate_ref[0, rows[p]].astype(F32)
        branch = jnp.concatenate([main_ref[0, rows[p]].astype(F32), mem_out], axis=-1)
        return (branch * (gate * jax.nn.sigmoid(gate))).astype(BF16)

    def project(p, branch):
        y = x_ref[0, rows[p]] + jnp.dot(branch, wout_ref[...], preferred_element_type=F32)
        if final_norm:
            y = _rmsnorm(y, fg_ref[...])
        o_ref[0, rows[p]] = y

    stages = (scores, softmax, values, gating, project)
    state = [None] * POST_SUBTILES
    for step in range(len(stages) + POST_SUBTILES - 1):
        for p in range(POST_SUBTILES):
            if 0 <= step - p < len(stages):
                state[p] = stages[step - p](p, state[p])


def _post(x, main, qm, gate, kbd, vbd, layer, w_out, final_g=None):
    b, s, _ = x.shape
    row = lambda i, j: (i, j, 0)
    per_batch = lambda i, j: (layer, i, 0, 0)
    const = lambda i, j: (0, 0)
    in_specs = [pl.BlockSpec((1, TM_POST, D_MODEL), row),
                pl.BlockSpec((1, TM_POST, MAIN_WIDTH), row),
                pl.BlockSpec((1, TM_POST, MEM_WIDTH), row),
                pl.BlockSpec((1, TM_POST, MIX_WIDTH), row),
                pl.BlockSpec((1, 1, MEM_WIDTH, MEM_HEADS * N_MEM), per_batch),
                pl.BlockSpec((1, 1, MEM_HEADS * N_MEM, MEM_WIDTH), per_batch),
                pl.BlockSpec((MIX_WIDTH, D_MODEL), const)]
    args = [x, main, qm, gate, kbd, vbd, w_out.astype(BF16)]
    if final_g is not None:
        in_specs.append(pl.BlockSpec((1, D_MODEL), const))
        args.append(final_g.reshape(1, -1))
    return pl.pallas_call(
        functools.partial(_post_kernel, final_norm=final_g is not None),
        grid=(b, s // TM_POST),
        in_specs=in_specs,
        out_specs=pl.BlockSpec((1, TM_POST, D_MODEL), row),
        out_shape=jax.ShapeDtypeStruct((b, s, D_MODEL), F32),
        compiler_params=_params(2),
        name="post_final" if final_g is not None else "post",
    )(*args)


def _fold_kernel(wf_ref, wn_ref, cc_ref, sc_ref, oa_ref, ob_ref):
    hp = lax.Precision.HIGHEST
    a = jnp.dot(cc_ref[...], wn_ref[0], precision=hp, preferred_element_type=F32)
    bm = jnp.dot(sc_ref[...], wn_ref[0], precision=hp, preferred_element_type=F32)
    oa_ref[0] = jnp.dot(wf_ref[0], a, precision=hp, preferred_element_type=F32)
    ob_ref[0] = jnp.dot(wf_ref[0], bm, precision=hp, preferred_element_type=F32)


def _channel_dft_tables(seq_len):
    n = FNET_GROUP_DIM
    jk = np.outer(np.arange(n), np.arange(n)) % n
    ang = 2.0 * np.pi * jk / n
    alpha = 1.0 / np.sqrt(float(seq_len) * n)
    cc = np.zeros((GROUP_PAD, GROUP_PAD), np.float32)
    sc = np.zeros((GROUP_PAD, GROUP_PAD), np.float32)
    cc[:n, :n] = alpha * np.cos(ang)
    sc[:n, :n] = -alpha * np.sin(ang)
    return jnp.asarray(cc), jnp.asarray(sc)


def _fold_fnet(w_in_f, w_fnet, seq_len):
    gpad = GROUP_PAD - FNET_GROUP_DIM
    wf = jnp.pad(w_in_f.reshape(D_MODEL, FNET_GROUPS, FNET_GROUP_DIM).transpose(1, 0, 2),
                 ((0, 0), (0, 0), (0, gpad)))
    wn = jnp.pad(w_fnet, ((0, 0), (0, gpad), (0, gpad)))
    cc, sc = _channel_dft_tables(seq_len)
    grp = lambda g: (g, 0, 0)
    const = lambda g: (0, 0)
    oa, ob = pl.pallas_call(
        _fold_kernel,
        grid=(FNET_GROUPS,),
        in_specs=[pl.BlockSpec((1, D_MODEL, GROUP_PAD), grp),
                  pl.BlockSpec((1, GROUP_PAD, GROUP_PAD), grp),
                  pl.BlockSpec((GROUP_PAD, GROUP_PAD), const),
                  pl.BlockSpec((GROUP_PAD, GROUP_PAD), const)],
        out_specs=[pl.BlockSpec((1, D_MODEL, GROUP_PAD), grp)] * 2,
        out_shape=[jax.ShapeDtypeStruct((FNET_GROUPS, D_MODEL, GROUP_PAD), F32)] * 2,
        compiler_params=_params(1),
        name="fnet_fold",
    )(wf, wn, cc, sc)
    unpack = lambda o: o[:, :, :FNET_GROUP_DIM].transpose(1, 0, 2).reshape(D_MODEL, MAIN_WIDTH)
    return unpack(oa), unpack(ob)


_L1_IN = 2 * MAIN_WIDTH + MEM_WIDTH + MIX_WIDTH


def _l1_pre_kernel(x_ref, g_ref, w_ref, u_ref, v_ref, qm_ref, gate_ref):
    h = _rmsnorm(x_ref[0], g_ref[...]).astype(BF16)
    proj = jnp.dot(h, w_ref[...], preferred_element_type=F32)
    u_ref[0] = proj[:, :MAIN_WIDTH].astype(BF16)
    v_ref[0] = proj[:, MAIN_WIDTH:2 * MAIN_WIDTH].astype(BF16)
    qm_ref[0] = proj[:, 2 * MAIN_WIDTH:2 * MAIN_WIDTH + MEM_WIDTH].astype(BF16)
    gate_ref[0] = proj[:, 2 * MAIN_WIDTH + MEM_WIDTH:].astype(BF16)


def _l1_pre(x, norm_g, w_all):
    b, s, _ = x.shape
    row = lambda i, j: (i, j, 0)
    const = lambda i, j: (0, 0)
    return pl.pallas_call(
        _l1_pre_kernel,
        grid=(b, s // TM_L1),
        in_specs=[pl.BlockSpec((1, TM_L1, D_MODEL), row),
                  pl.BlockSpec((1, D_MODEL), const),
                  pl.BlockSpec((D_MODEL, _L1_IN), const, pipeline_mode=pl.Buffered(1))],
        out_specs=[pl.BlockSpec((1, TM_L1, MAIN_WIDTH), row),
                   pl.BlockSpec((1, TM_L1, MAIN_WIDTH), row),
                   pl.BlockSpec((1, TM_L1, MEM_WIDTH), row),
                   pl.BlockSpec((1, TM_L1, MIX_WIDTH), row)],
        out_shape=[jax.ShapeDtypeStruct((b, s, MAIN_WIDTH), BF16),
                   jax.ShapeDtypeStruct((b, s, MAIN_WIDTH), BF16),
                   jax.ShapeDtypeStruct((b, s, MEM_WIDTH), BF16),
                   jax.ShapeDtypeStruct((b, s, MIX_WIDTH), BF16)],
        compiler_params=_params(2),
        name="l1_pre",
    )(x, norm_g.reshape(1, -1), w_all)


def _dft3_tables():
    r = FFT_R
    i = np.arange(r)
    w = lambda m, e: np.exp(-2j * np.pi * (e % m) / m)
    f16 = w(r, np.outer(i, i))
    eye = np.eye(r)
    ga = np.einsum("ed,ka,bk->bekad", eye, f16, w(r * r, np.outer(i, i))).reshape(r, FFT_G, FFT_G)
    tw = w(r ** 3, i[:, None, None] * (i[None, None, :] + r * i[None, :, None]))
    gb = np.einsum("jk,qb,dqj->dqjbk", eye, f16, tw).reshape(r, FFT_G, FFT_G)
    gc = np.einsum("jk,qd->qjdk", eye, f16).reshape(FFT_G, FFT_G)
    stack = lambda g: np.concatenate([np.concatenate([g.real, -g.imag], axis=-1),
                                      np.concatenate([g.imag, g.real], axis=-1)], axis=-2)
    real_rows = np.concatenate([gc.real, -gc.imag], axis=-1)
    f32 = lambda a: jnp.asarray(a.astype(np.float32))
    return f32(stack(ga)), f32(stack(gb)), f32(real_rows)


def _dft3_kernel(u_ref, v_ref, ga_ref, gb_ref, gc_ref, o_ref, yr_ref, yi_ref):
    r, g = FFT_R, FFT_G
    tiles = (r, r, FFT_CW)

    def apply(mat, zr, zi):
        z = jnp.concatenate([zr.reshape(g, FFT_CW), zi.reshape(g, FFT_CW)], axis=0)
        return jnp.dot(mat, z, preferred_element_type=F32)

    def stage_a(b, carry):
        y = apply(ga_ref[b], u_ref[0, :, b], v_ref[0, :, b])
        yr_ref[:, b] = y[:g].astype(BF16).reshape(tiles)
        yi_ref[:, b] = y[g:].astype(BF16).reshape(tiles)
        return carry

    def stage_b(d, carry):
        y = apply(gb_ref[d], yr_ref[d], yi_ref[d])
        yr_ref[d] = y[:g].astype(BF16).reshape(tiles)
        yi_ref[d] = y[g:].astype(BF16).reshape(tiles)
        return carry

    def stage_c(kb, carry):
        y = apply(gc_ref[...], yr_ref[:, kb], yi_ref[:, kb])
        o_ref[0, :, kb] = y.astype(BF16).reshape(tiles)
        return carry

    lax.fori_loop(0, r, stage_a, 0, unroll=FFT_UNROLL)
    lax.fori_loop(0, r, stage_b, 0, unroll=FFT_UNROLL)
    lax.fori_loop(0, r, stage_c, 0, unroll=FFT_UNROLL)


def _seq_dft_real(u, v):
    b, s, w = u.shape
    r = FFT_R
    assert s == r ** 3 and w % FFT_CW == 0
    ga, gb, gc = (t.astype(BF16) for t in _dft3_tables())
    shape5 = (b, r, r, r, w)
    blk = pl.BlockSpec((1, r, r, r, FFT_CW), lambda i, c: (i, 0, 0, 0, c))
    out = pl.pallas_call(
        _dft3_kernel,
        grid=(b, w // FFT_CW),
        in_specs=[blk, blk,
                  pl.BlockSpec((r, 2 * FFT_G, 2 * FFT_G), lambda i, c: (0, 0, 0), pipeline_mode=pl.Buffered(1)),
                  pl.BlockSpec((r, 2 * FFT_G, 2 * FFT_G), lambda i, c: (0, 0, 0), pipeline_mode=pl.Buffered(1)),
                  pl.BlockSpec((FFT_G, 2 * FFT_G), lambda i, c: (0, 0), pipeline_mode=pl.Buffered(1))],
        out_specs=blk,
        out_shape=jax.ShapeDtypeStruct(shape5, BF16),
        scratch_shapes=[pltpu.VMEM((r, r, r, FFT_CW), BF16), pltpu.VMEM((r, r, r, FFT_CW), BF16)],
        compiler_params=_params(2),
        name="seq_dft",
    )(u.reshape(shape5), v.reshape(shape5), ga, gb, gc)
    return out.reshape(b, s, w)


def kernel(x, mem, positions, norm_g_l0, w_in_l0, q_norm_g_l0, kv_norm_g_l0, w_uq_l0, w_ukv_l0,
           mem_norm_g_l0, w_mem_kv_l0, w_out_l0, norm_g_l1, w_in_l1, w_fnet_l1, mem_norm_g_l1,
           w_mem_kv_l1, w_out_l1, final_norm_g):
    s = x.shape[1]
    cos_t, sin_t = _rope_tables(positions)
    kbd, vbd = _mem_kv(mem, jnp.stack([mem_norm_g_l0, mem_norm_g_l1]),
                       jnp.stack([w_mem_kv_l0, w_mem_kv_l1]))

    qt, k, vt, qm0, gate0 = _l0_pre(x, cos_t, sin_t, norm_g_l0, w_in_l0, q_norm_g_l0, kv_norm_g_l0,
                                    w_uq_l0, w_ukv_l0)
    attn = _attention(qt, k, vt)
    x1 = _post(x, attn, qm0, gate0, kbd, vbd, 0, w_out_l0)

    w_a, w_b = _fold_fnet(w_in_l1[:, :MAIN_WIDTH], w_fnet_l1, s)
    w_all = jnp.concatenate([w_a, w_b, w_in_l1[:, MAIN_WIDTH:]], axis=1).astype(BF16)
    u, v, qm1, gate1 = _l1_pre(x1, norm_g_l1, w_all)
    mixed = _seq_dft_real(u, v)
    return _post(x1, mixed, qm1, gate1, kbd, vbd, 1, w_out_l1, final_g=final_norm_g)
```

```python
import functools

import numpy as np
import jax
import jax.numpy as jnp
from jax import lax
from jax.experimental import pallas as pl
from jax.experimental.pallas import tpu as pltpu

F32 = jnp.float32
BF16 = jnp.bfloat16

D_MODEL = 1024
N_MEM = 256
MIX_WIDTH = D_MODEL
MEM_HEADS = 4
MEM_WIDTH = MIX_WIDTH // 4
MEM_HEAD_DIM = MEM_WIDTH // MEM_HEADS
MAIN_WIDTH = MIX_WIDTH - MEM_WIDTH
MLA_HEADS = 12
QK_NOPE = 64
QK_ROPE = 32
V_HEAD = MAIN_WIDTH // MLA_HEADS
Q_LORA = (3 * D_MODEL) // 8
KV_LORA = D_MODEL // 4
ROPE_THETA = 10000.0
FNET_GROUPS = 4
FNET_GROUP_DIM = MAIN_WIDTH // FNET_GROUPS
EPS = 1e-6

LANES = 128
HEAD_PAD = LANES
QK_PAD = MLA_HEADS * HEAD_PAD
QK_DIM = QK_NOPE + QK_ROPE
ROPE_LO = QK_NOPE
ROPE_HALF = QK_ROPE // 2
GROUP_PAD = 256

FFT_R = 16
FFT_G = FFT_R * FFT_R
FFT_CW = 256
FFT_UNROLL = 16

TM = 1024
TQ = 512
TKV = 256
TM_L1 = 1024
TM_POST = 1024
POST_SUBTILES = 4

VMEM_LIMIT = 56 * 1024 * 1024


def _rmsnorm(x, g):
    return x * lax.rsqrt(jnp.mean(x * x, axis=-1, keepdims=True) + EPS) * g


def _params(n_grid_dims):
    return pltpu.CompilerParams(
        dimension_semantics=("arbitrary",) * n_grid_dims,
        vmem_limit_bytes=VMEM_LIMIT,
    )


def _rope_table_kernel(pos_ref, invf_ref, cos_ref, sin_ref):
    ang = invf_ref[...] * pos_ref[0].astype(F32)
    c, s = jnp.cos(ang), jnp.sin(ang)
    cos_ref[0] = jnp.concatenate([c, c], axis=0)
    sin_ref[0] = jnp.concatenate([-s, s], axis=0)


def _rope_tables(positions):
    b, s = positions.shape
    inv_freq = 1.0 / (ROPE_THETA ** (jnp.arange(0, QK_ROPE, 2, dtype=F32) / QK_ROPE))
    invf = inv_freq.reshape(ROPE_HALF, 1)
    return pl.pallas_call(
        _rope_table_kernel,
        grid=(b,),
        in_specs=[pl.BlockSpec((1, 1, s), lambda i: (i, 0, 0)),
                  pl.BlockSpec((ROPE_HALF, 1), lambda i: (0, 0))],
        out_specs=[pl.BlockSpec((1, QK_ROPE, s), lambda i: (i, 0, 0))] * 2,
        out_shape=[jax.ShapeDtypeStruct((b, QK_ROPE, s), F32)] * 2,
        compiler_params=_params(1),
        name="rope_tables",
    )(positions.reshape(b, 1, s), invf)


def _mem_kv_kernel(mem_ref, g_ref, w_ref, kbd_ref, vbd_ref):
    mn = _rmsnorm(mem_ref[0], g_ref[0]).astype(BF16)
    kv = jnp.dot(mn, w_ref[0], preferred_element_type=F32)
    k_t = (kv[:, :MEM_WIDTH] * (1.0 / float(np.sqrt(MEM_HEAD_DIM)))).T
    v = kv[:, MEM_WIDTH:]
    k_rep = jnp.concatenate([k_t] * MEM_HEADS, axis=1)
    r = lax.broadcasted_iota(jnp.int32, k_rep.shape, 0) // MEM_HEAD_DIM
    c = lax.broadcasted_iota(jnp.int32, k_rep.shape, 1) // N_MEM
    kbd_ref[0, 0] = jnp.where(r == c, k_rep, 0.0).astype(BF16)
    v_rep = jnp.concatenate([v] * MEM_HEADS, axis=0)
    r = lax.broadcasted_iota(jnp.int32, v_rep.shape, 0) // N_MEM
    c = lax.broadcasted_iota(jnp.int32, v_rep.shape, 1) // MEM_HEAD_DIM
    vbd_ref[0, 0] = jnp.where(r == c, v_rep, 0.0).astype(BF16)


def _mem_kv(mem, gains, weights):
    b = mem.shape[0]
    n_layers = gains.shape[0]
    per_layer = lambda i, l: (l, 0, 0)
    out = lambda i, l: (l, i, 0, 0)
    return pl.pallas_call(
        _mem_kv_kernel,
        grid=(b, n_layers),
        in_specs=[pl.BlockSpec((1, N_MEM, D_MODEL), lambda i, l: (i, 0, 0)),
                  pl.BlockSpec((1, 1, D_MODEL), per_layer),
                  pl.BlockSpec((1, D_MODEL, 2 * MEM_WIDTH), per_layer)],
        out_specs=[pl.BlockSpec((1, 1, MEM_WIDTH, MEM_HEADS * N_MEM), out),
                   pl.BlockSpec((1, 1, MEM_HEADS * N_MEM, MEM_WIDTH), out)],
        out_shape=[jax.ShapeDtypeStruct((n_layers, b, MEM_WIDTH, MEM_HEADS * N_MEM), BF16),
                   jax.ShapeDtypeStruct((n_layers, b, MEM_HEADS * N_MEM, MEM_WIDTH), BF16)],
        compiler_params=_params(2),
        name="mem_kv",
    )(mem, gains.reshape(n_layers, 1, D_MODEL), weights.astype(BF16))


_L0_CQ = (0, Q_LORA)
_L0_CKV = (_L0_CQ[1], _L0_CQ[1] + KV_LORA)
_L0_QM = (_L0_CKV[1], _L0_CKV[1] + MEM_WIDTH)
_L0_GATE = (_L0_QM[1], _L0_QM[1] + MIX_WIDTH)
_L0_KR = (_L0_GATE[1], _L0_GATE[1] + HEAD_PAD)
_L0_IN = _L0_KR[1]


def _l0_pre_kernel(x_ref, cos_ref, sin_ref, g_ref, w_in_ref, gq_ref, gkv_ref, wqt_ref, wk_ref,
                   wvt_ref, qt_ref, k_ref, vt_ref, qm_ref, gate_ref):
    h = _rmsnorm(x_ref[0], g_ref[...]).astype(BF16)
    proj = jnp.dot(h, w_in_ref[...], preferred_element_type=F32)
    qm_ref[0] = proj[:, _L0_QM[0]:_L0_QM[1]].astype(BF16)
    gate_ref[0] = proj[:, _L0_GATE[0]:_L0_GATE[1]].astype(BF16)
    cqn = _rmsnorm(proj[:, _L0_CQ[0]:_L0_CQ[1]], gq_ref[...]).astype(BF16)
    ckvn = _rmsnorm(proj[:, _L0_CKV[0]:_L0_CKV[1]], gkv_ref[...]).astype(BF16)
    nt = (((1,), (1,)), ((), ()))
    qt = lax.dot_general(wqt_ref[...], cqn, nt, preferred_element_type=F32)
    kn = jnp.dot(ckvn, wk_ref[...], preferred_element_type=F32)
    vt = lax.dot_general(wvt_ref[...], ckvn, nt, preferred_element_type=F32)
    vt_ref[0] = vt.astype(BF16)

    cos_t = cos_ref[0]
    sin_t = sin_ref[0]

    def rope_t(blk):
        partner = jnp.concatenate([blk[ROPE_HALF:], blk[:ROPE_HALF]], axis=0)
        return blk * cos_t + partner * sin_t

    rope_rows = slice(ROPE_LO, ROPE_LO + QK_ROPE)
    kr_t = proj[:, _L0_KR[0]:_L0_KR[1]].T
    k_rope = jnp.concatenate([kr_t[:ROPE_LO], rope_t(kr_t[rope_rows]), kr_t[ROPE_LO + QK_ROPE:]],
                             axis=0).T
    scale = float(np.log2(np.e)) / float(np.sqrt(QK_NOPE + QK_ROPE))
    nope_lane = lax.broadcasted_iota(jnp.int32, k_rope.shape, 1) < QK_NOPE
    for hd in range(MLA_HEADS):
        sl = slice(hd * HEAD_PAD, (hd + 1) * HEAD_PAD)
        rows = slice(hd * QK_DIM, (hd + 1) * QK_DIM)
        q_h = qt[rows]
        q_h = jnp.concatenate([q_h[:ROPE_LO], rope_t(q_h[rope_rows])], axis=0)
        qt_ref[0, rows, :] = (q_h * scale).astype(BF16)
        pair = kn[:, (hd // 2) * HEAD_PAD:(hd // 2 + 1) * HEAD_PAD]
        if hd % 2:
            pair = pltpu.roll(pair, QK_NOPE, 1)
        k_ref[0, :, sl] = jnp.where(nope_lane, pair, k_rope).astype(BF16)


def _l0_pre(x, cos_t, sin_t, norm_g, w_in, q_norm_g, kv_norm_g, w_uq, w_ukv):
    b, s, _ = x.shape
    o1, o2, o3, o4 = Q_LORA, Q_LORA + KV_LORA, Q_LORA + KV_LORA + QK_ROPE, Q_LORA + KV_LORA + QK_ROPE + MEM_WIDTH
    w_kr = jnp.pad(w_in[:, o2:o3], ((0, 0), (ROPE_LO, HEAD_PAD - ROPE_LO - QK_ROPE)))
    w_in_p = jnp.concatenate([w_in[:, :o2], w_in[:, o3:o4], w_in[:, o4:], w_kr], axis=1).astype(BF16)
    wqt = w_uq.T.astype(BF16)
    w_ukv3 = w_ukv.reshape(KV_LORA, MLA_HEADS, QK_NOPE + V_HEAD)
    wk = w_ukv3[:, :, :QK_NOPE].reshape(KV_LORA, MLA_HEADS * QK_NOPE).astype(BF16)
    wvt = w_ukv3[:, :, QK_NOPE:].reshape(KV_LORA, MAIN_WIDTH).T.astype(BF16)

    row = lambda i, j: (i, j, 0)
    col = lambda i, j: (i, 0, j)
    const = lambda i, j: (0, 0)
    return pl.pallas_call(
        _l0_pre_kernel,
        grid=(b, s // TM),
        in_specs=[pl.BlockSpec((1, TM, D_MODEL), row),
                  pl.BlockSpec((1, QK_ROPE, TM), col),
                  pl.BlockSpec((1, QK_ROPE, TM), col),
                  pl.BlockSpec((1, D_MODEL), const),
                  pl.BlockSpec((D_MODEL, _L0_IN), const, pipeline_mode=pl.Buffered(1)),
                  pl.BlockSpec((1, Q_LORA), const),
                  pl.BlockSpec((1, KV_LORA), const),
                  pl.BlockSpec((MLA_HEADS * QK_DIM, Q_LORA), const, pipeline_mode=pl.Buffered(1)),
                  pl.BlockSpec((KV_LORA, MLA_HEADS * QK_NOPE), const, pipeline_mode=pl.Buffered(1)),
                  pl.BlockSpec((MAIN_WIDTH, KV_LORA), const, pipeline_mode=pl.Buffered(1))],
        out_specs=[pl.BlockSpec((1, MLA_HEADS * QK_DIM, TM), col),
                   pl.BlockSpec((1, TM, QK_PAD), row),
                   pl.BlockSpec((1, MAIN_WIDTH, TM), col),
                   pl.BlockSpec((1, TM, MEM_WIDTH), row),
                   pl.BlockSpec((1, TM, MIX_WIDTH), row)],
        out_shape=[jax.ShapeDtypeStruct((b, MLA_HEADS * QK_DIM, s), BF16),
                   jax.ShapeDtypeStruct((b, s, QK_PAD), BF16),
                   jax.ShapeDtypeStruct((b, MAIN_WIDTH, s), BF16),
                   jax.ShapeDtypeStruct((b, s, MEM_WIDTH), BF16),
                   jax.ShapeDtypeStruct((b, s, MIX_WIDTH), BF16)],
        compiler_params=_params(2),
        name="l0_pre",
    )(x, cos_t, sin_t, norm_g.reshape(1, -1), w_in_p, q_norm_g.reshape(1, -1),
      kv_norm_g.reshape(1, -1), wqt, wk, wvt)


HEADS_PER_STEP = 2
SUM_ROWS = 16
SCORE_AHEAD = 1
SCORE_SLOTS = 2


def _attn_kernel(qt_ref, k_ref, vt_ref, o_ref, s_ref, mx_ref, m_ref, acc_ref):
    n_tiles = k_ref.shape[1] // TQ
    n_chunks = k_ref.shape[1] // TKV
    assert n_chunks % SCORE_SLOTS == 0

    def scores(hd, tile, chunk, slot):
        q_off = pl.multiple_of(tile * TQ, TQ)
        k_off = pl.multiple_of(chunk * TKV, TKV)
        lanes = slice(hd * HEAD_PAD, (hd + 1) * HEAD_PAD)
        q_t = jnp.concatenate([qt_ref[0, hd * QK_DIM:(hd + 1) * QK_DIM, pl.ds(q_off, TQ)],
                               jnp.zeros((HEAD_PAD - QK_DIM, TQ), BF16)], axis=0)
        s_t = jnp.dot(k_ref[0, pl.ds(k_off, TKV), lanes], q_t, preferred_element_type=F32)
        s_ref[slot, hd] = s_t
        mx_ref[slot, hd] = jnp.max(s_t, axis=0, keepdims=True)

    def consume(hd, chunk, slot):
        k_off = pl.multiple_of(chunk * TKV, TKV)
        m_old = m_ref[hd]
        m_new = jnp.maximum(m_old, mx_ref[slot, hd])
        alpha = jnp.exp2(m_old - m_new)
        p = jnp.exp2(s_ref[slot, hd] - m_new).astype(BF16)
        m_ref[hd] = m_new
        v_aug = jnp.concatenate([vt_ref[0, hd * V_HEAD:(hd + 1) * V_HEAD, pl.ds(k_off, TKV)],
                                 jnp.ones((SUM_ROWS, TKV), BF16)], axis=0)
        acc_ref[hd] = alpha * acc_ref[hd] + jnp.dot(v_aug, p, preferred_element_type=F32)

    def tile(t, carry):
        m_ref[...] = jnp.full(m_ref.shape, -jnp.inf, F32)
        acc_ref[...] = jnp.zeros(acc_ref.shape, F32)
        for c in range(n_chunks):
            ahead = c + SCORE_AHEAD
            nxt = (t, ahead) if ahead < n_chunks else (jnp.minimum(t + 1, n_tiles - 1), ahead - n_chunks)
            for hd in range(HEADS_PER_STEP):
                scores(hd, *nxt, ahead % SCORE_SLOTS)
                consume(hd, c, c % SCORE_SLOTS)
        out =[acc_ref[hd, :V_HEAD, :] * (1.0 / acc_ref[hd, V_HEAD:V_HEAD + 1, :])
               for hd in range(HEADS_PER_STEP)]
        o_ref[0, pl.ds(pl.multiple_of(t * TQ, TQ), TQ), :] = jnp.concatenate(out, axis=0).T.astype(BF16)
        return carry

    for c in range(SCORE_AHEAD):
        for hd in range(HEADS_PER_STEP):
            scores(hd, 0, c, c)
    lax.fori_loop(0, n_tiles, tile, 0)


def _attention(qt, k, vt):
    b, s, _ = k.shape
    width = HEADS_PER_STEP * HEAD_PAD
    vrows = HEADS_PER_STEP * V_HEAD
    return pl.pallas_call(
        _attn_kernel,
        grid=(b, MLA_HEADS // HEADS_PER_STEP),
        in_specs=[pl.BlockSpec((1, HEADS_PER_STEP * QK_DIM, s), lambda i, j: (i, j, 0)),
                  pl.BlockSpec((1, s, width), lambda i, j: (i, 0, j)),
                  pl.BlockSpec((1, vrows, s), lambda i, j: (i, j, 0))],
        out_specs=pl.BlockSpec((1, s, vrows), lambda i, j: (i, 0, j)),
        out_shape=jax.ShapeDtypeStruct((b, s, MAIN_WIDTH), BF16),
        scratch_shapes=[pltpu.VMEM((SCORE_SLOTS, HEADS_PER_STEP, TKV, TQ), F32),
                        pltpu.VMEM((SCORE_SLOTS, HEADS_PER_STEP, 1, TQ), F32),
                        pltpu.VMEM((HEADS_PER_STEP, 1, TQ), F32),
                        pltpu.VMEM((HEADS_PER_STEP, V_HEAD + SUM_ROWS, TQ), F32)],
        compiler_params=_params(2),
        name="mla_attention",
    )(qt, k, vt)


def _post_kernel(*refs, final_norm):
    if final_norm:
        x_ref, main_ref, qm_ref, gate_ref, kbd_ref, vbd_ref, wout_ref, fg_ref, o_ref = refs
    else:
        x_ref, main_ref, qm_ref, gate_ref, kbd_ref, vbd_ref, wout_ref, o_ref = refs
    sub = x_ref.shape[1] // POST_SUBTILES
    rows = [slice(p * sub, (p + 1) * sub) for p in range(POST_SUBTILES)]

    def scores(p, _):
        return jnp.dot(qm_ref[0, rows[p]], kbd_ref[0, 0], preferred_element_type=F32)

    def softmax(p, s):
        probs = []
        for hd in range(MEM_HEADS):
            sh = s[:, hd * N_MEM:(hd + 1) * N_MEM]
            e = jnp.exp(sh - jnp.max(sh, axis=-1, keepdims=True))
            probs.append((e * (1.0 / jnp.sum(e, axis=-1, keepdims=True))).astype(BF16))
        return jnp.concatenate(probs, axis=-1)

    def values(p, probs):
        return jnp.dot(probs, vbd_ref[0, 0], preferred_element_type=F32)

    def gating(p, mem_out):
        gate = gate_ref[0, rows[p]].astype(F32)
        branch = jnp.concatenate([main_ref[0, rows[p]].astype(F32), mem_out], axis=-1)
        return (branch * (0.5 * gate * (1.0 + jnp.tanh(0.5 * gate)))).astype(BF16)

    def project(p, branch):
        y = x_ref[0, rows[p]] + jnp.dot(branch, wout_ref[...], preferred_element_type=F32)
        if final_norm:
            y = _rmsnorm(y, fg_ref[...])
        o_ref[0, rows[p]] = y

    stages = (scores, softmax, values, gating, project)
    state = [None] * POST_SUBTILES
    for step in range(len(stages) + POST_SUBTILES - 1):
        for p in range(POST_SUBTILES):
            if 0 <= step - p < len(stages):
                state[p] = stages[step - p](p, state[p])


def _post(x, main, qm, gate, kbd, vbd, layer, w_out, final_g=None):
    b, s, _ = x.shape
    row = lambda i, j: (i, j, 0)
    per_batch = lambda i, j: (layer, i, 0, 0)
    const = lambda i, j: (0, 0)
    in_specs = [pl.BlockSpec((1, TM_POST, D_MODEL), row),
                pl.BlockSpec((1, TM_POST, MAIN_WIDTH), row),
                pl.BlockSpec((1, TM_POST, MEM_WIDTH), row),
                pl.BlockSpec((1, TM_POST, MIX_WIDTH), row),
                pl.BlockSpec((1, 1, MEM_WIDTH, MEM_HEADS * N_MEM), per_batch),
                pl.BlockSpec((1, 1, MEM_HEADS * N_MEM, MEM_WIDTH), per_batch),
                pl.BlockSpec((MIX_WIDTH, D_MODEL), const)]
    args = [x, main, qm, gate, kbd, vbd, w_out.astype(BF16)]
    if final_g is not None:
        in_specs.append(pl.BlockSpec((1, D_MODEL), const))
        args.append(final_g.reshape(1, -1))
    return pl.pallas_call(
        functools.partial(_post_kernel, final_norm=final_g is not None),
        grid=(b, s // TM_POST),
        in_specs=in_specs,
        out_specs=pl.BlockSpec((1, TM_POST, D_MODEL), row),
        out_shape=jax.ShapeDtypeStruct((b, s, D_MODEL), F32),
        compiler_params=_params(2),
        name="post_final" if final_g is not None else "post",
    )(*args)


def _fold_kernel(wf_ref, wn_ref, cc_ref, sc_ref, oa_ref, ob_ref):
    hp = lax.Precision.HIGHEST
    a = jnp.dot(cc_ref[...], wn_ref[0], precision=hp, preferred_element_type=F32)
    bm = jnp.dot(sc_ref[...], wn_ref[0], precision=hp, preferred_element_type=F32)
    oa_ref[0] = jnp.dot(wf_ref[0], a, precision=hp, preferred_element_type=F32)
    ob_ref[0] = jnp.dot(wf_ref[0], bm, precision=hp, preferred_element_type=F32)


def _channel_dft_tables(seq_len):
    n = FNET_GROUP_DIM
    jk = np.outer(np.arange(n), np.arange(n)) % n
    ang = 2.0 * np.pi * jk / n
    alpha = 1.0 / np.sqrt(float(seq_len) * n)
    cc = np.zeros((GROUP_PAD, GROUP_PAD), np.float32)
    sc = np.zeros((GROUP_PAD, GROUP_PAD), np.float32)
    cc[:n, :n] = alpha * np.cos(ang)
    sc[:n, :n] = -alpha * np.sin(ang)
    return jnp.asarray(cc), jnp.asarray(sc)


def _fold_fnet(w_in_f, w_fnet, seq_len):
    gpad = GROUP_PAD - FNET_GROUP_DIM
    wf = jnp.pad(w_in_f.reshape(D_MODEL, FNET_GROUPS, FNET_GROUP_DIM).transpose(1, 0, 2),
                 ((0, 0), (0, 0), (0, gpad)))
    wn = jnp.pad(w_fnet, ((0, 0), (0, gpad), (0, gpad)))
    cc, sc = _channel_dft_tables(seq_len)
    grp = lambda g: (g, 0, 0)
    const = lambda g: (0, 0)
    oa, ob = pl.pallas_call(
        _fold_kernel,
        grid=(FNET_GROUPS,),
        in_specs=[pl.BlockSpec((1, D_MODEL, GROUP_PAD), grp),
                  pl.BlockSpec((1, GROUP_PAD, GROUP_PAD), grp),
                  pl.BlockSpec((GROUP_PAD, GROUP_PAD), const),
                  pl.BlockSpec((GROUP_PAD, GROUP_PAD), const)],
        out_specs=[pl.BlockSpec((1, D_MODEL, GROUP_PAD), grp)] * 2,
        out_shape=[jax.ShapeDtypeStruct((FNET_GROUPS, D_MODEL, GROUP_PAD), F32)] * 2,
        compiler_params=_params(1),
        name="fnet_fold",
    )(wf, wn, cc, sc)
    unpack = lambda o: o[:, :, :FNET_GROUP_DIM].transpose(1, 0, 2).reshape(D_MODEL, MAIN_WIDTH)
    return unpack(oa), unpack(ob)


_L1_IN = 2 * MAIN_WIDTH + MEM_WIDTH + MIX_WIDTH


def _l1_pre_kernel(x_ref, g_ref, w_ref, u_ref, v_ref, qm_ref, gate_ref):
    h = _rmsnorm(x_ref[0], g_ref[...]).astype(BF16)
    proj = jnp.dot(h, w_ref[...], preferred_element_type=F32)
    u_ref[0] = proj[:, :MAIN_WIDTH].astype(BF16)
    v_ref[0] = proj[:, MAIN_WIDTH:2 * MAIN_WIDTH].astype(BF16)
    qm_ref[0] = proj[:, 2 * MAIN_WIDTH:2 * MAIN_WIDTH + MEM_WIDTH].astype(BF16)
    gate_ref[0] = proj[:, 2 * MAIN_WIDTH + MEM_WIDTH:].astype(BF16)


def _l1_pre(x, norm_g, w_all):
    b, s, _ = x.shape
    row = lambda i, j: (i, j, 0)
    const = lambda i, j: (0, 0)
    return pl.pallas_call(
        _l1_pre_kernel,
        grid=(b, s // TM_L1),
        in_specs=[pl.BlockSpec((1, TM_L1, D_MODEL), row),
                  pl.BlockSpec((1, D_MODEL), const),
                  pl.BlockSpec((D_MODEL, _L1_IN), const, pipeline_mode=pl.Buffered(1))],
        out_specs=[pl.BlockSpec((1, TM_L1, MAIN_WIDTH), row),
                   pl.BlockSpec((1, TM_L1, MAIN_WIDTH), row),
                   pl.BlockSpec((1, TM_L1, MEM_WIDTH), row),
                   pl.BlockSpec((1, TM_L1, MIX_WIDTH), row)],
        out_shape=[jax.ShapeDtypeStruct((b, s, MAIN_WIDTH), BF16),
                   jax.ShapeDtypeStruct((b, s, MAIN_WIDTH), BF16),
                   jax.ShapeDtypeStruct((b, s, MEM_WIDTH), BF16),
                   jax.ShapeDtypeStruct((b, s, MIX_WIDTH), BF16)],
        compiler_params=_params(2),
        name="l1_pre",
    )(x, norm_g.reshape(1, -1), w_all)


def _dft3_tables():
    r = FFT_R
    i = np.arange(r)
    w = lambda m, e: np.exp(-2j * np.pi * (e % m) / m)
    f16 = w(r, np.outer(i, i))
    eye = np.eye(r)
    ga = np.einsum("ed,ka,bk->bekad", eye, f16, w(r * r, np.outer(i, i))).reshape(r, FFT_G, FFT_G)
    tw = w(r ** 3, i[:, None, None] * (i[None, None, :] + r * i[None, :, None]))
    gb = np.einsum("jk,qb,dqj->dqjbk", eye, f16, tw).reshape(r, FFT_G, FFT_G)
    gc = np.einsum("jk,qd->qjdk", eye, f16).reshape(FFT_G, FFT_G)
    stack = lambda g: np.concatenate([np.concatenate([g.real, -g.imag], axis=-1),
                                      np.concatenate([g.imag, g.real], axis=-1)], axis=-2)
    real_rows = np.concatenate([gc.real, -gc.imag], axis=-1)
    f32 = lambda a: jnp.asarray(a.astype(np.float32))
    return f32(stack(ga)), f32(stack(gb)), f32(real_rows)


def _dft3_kernel(u_ref, v_ref, ga_ref, gb_ref, gc_ref, o_ref, yr_ref, yi_ref):
    r, g = FFT_R, FFT_G
    tiles = (r, r, FFT_CW)

    def apply(mat, zr, zi):
        z = jnp.concatenate([zr.reshape(g, FFT_CW), zi.reshape(g, FFT_CW)], axis=0)
        return jnp.dot(mat, z, preferred_element_type=F32)

    def stage_a(b, carry):
        y = apply(ga_ref[b], u_ref[0, :, b], v_ref[0, :, b])
        yr_ref[:, b] = y[:g].astype(BF16).reshape(tiles)
        yi_ref[:, b] = y[g:].astype(BF16).reshape(tiles)
        return carry

    def stage_b(d, carry):
        y = apply(gb_ref[d], yr_ref[d], yi_ref[d])
        yr_ref[d] = y[:g].astype(BF16).reshape(tiles)
        yi_ref[d] = y[g:].astype(BF16).reshape(tiles)
        return carry

    def stage_c(kb, carry):
        y = apply(gc_ref[...], yr_ref[:, kb], yi_ref[:, kb])
        o_ref[0, :, kb] = y.astype(BF16).reshape(tiles)
        return carry

    lax.fori_loop(0, r, stage_a, 0, unroll=FFT_UNROLL)
    lax.fori_loop(0, r, stage_b, 0, unroll=FFT_UNROLL)
    lax.fori_loop(0, r, stage_c, 0, unroll=FFT_UNROLL)


def _seq_dft_real(u, v):
    b, s, w = u.shape
    r = FFT_R
    assert s == r ** 3 and w % FFT_CW == 0
    ga, gb, gc = (t.astype(BF16) for t in _dft3_tables())
    shape5 = (b, r, r, r, w)
    blk = pl.BlockSpec((1, r, r, r, FFT_CW), lambda i, c: (i, 0, 0, 0, c))
    out = pl.pallas_call(
        _dft3_kernel,
        grid=(b, w // FFT_CW),
        in_specs=[blk, blk,
                  pl.BlockSpec((r, 2 * FFT_G, 2 * FFT_G), lambda i, c: (0, 0, 0), pipeline_mode=pl.Buffered(1)),
                  pl.BlockSpec((r, 2 * FFT_G, 2 * FFT_G), lambda i, c: (0, 0, 0), pipeline_mode=pl.Buffered(1)),
                  pl.BlockSpec((FFT_G, 2 * FFT_G), lambda i, c: (0, 0), pipeline_mode=pl.Buffered(1))],
        out_specs=blk,
        out_shape=jax.ShapeDtypeStruct(shape5, BF16),
        scratch_shapes=[pltpu.VMEM((r, r, r, FFT_CW), BF16), pltpu.VMEM((r, r, r, FFT_CW), BF16)],
        compiler_params=_params(2),
        name="seq_dft",
    )(u.reshape(shape5), v.reshape(shape5), ga, gb, gc)
    return out.reshape(b, s, w)


def kernel(x, mem, positions, norm_g_l0, w_in_l0, q_norm_g_l0, kv_norm_g_l0, w_uq_l0, w_ukv_l0,
           mem_norm_g_l0, w_mem_kv_l0, w_out_l0, norm_g_l1, w_in_l1, w_fnet_l1, mem_norm_g_l1,
           w_mem_kv_l1, w_out_l1, final_norm_g):
    s = x.shape[1]
    cos_t, sin_t = _rope_tables(positions)
    kbd, vbd = _mem_kv(mem, jnp.stack([mem_norm_g_l0, mem_norm_g_l1]),
                       jnp.stack([w_mem_kv_l0, w_mem_kv_l1]))

    qt, k, vt, qm0, gate0 = _l0_pre(x, cos_t, sin_t, norm_g_l0, w_in_l0, q_norm_g_l0, kv_norm_g_l0,
                                    w_uq_l0, w_ukv_l0)
    attn = _attention(qt, k, vt)
    x1 = _post(x, attn, qm0, gate0, kbd, vbd, 0, w_out_l0)

    w_a, w_b = _fold_fnet(w_in_l1[:, :MAIN_WIDTH], w_fnet_l1, s)
    w_all = jnp.concatenate([w_a, w_b, w_in_l1[:, MAIN_WIDTH:]], axis=1).astype(BF16)
    u, v, qm1, gate1 = _l1_pre(x1, norm_g_l1, w_all)
    mixed = _seq_dft_real(u, v)
    return _post(x1, mixed, qm1, gate1, kbd, vbd, 1, w_out_l1, final_g=final_norm_g)
```

```python
import functools

import numpy as np
import jax
import jax.numpy as jnp
from jax import lax
from jax.experimental import pallas as pl
from jax.experimental.pallas import tpu as pltpu

F32 = jnp.float32
BF16 = jnp.bfloat16

D_MODEL = 1024
N_MEM = 256
MIX_WIDTH = D_MODEL
MEM_HEADS = 4
MEM_WIDTH = MIX_WIDTH // 4
MEM_HEAD_DIM = MEM_WIDTH // MEM_HEADS
MAIN_WIDTH = MIX_WIDTH - MEM_WIDTH
MLA_HEADS = 12
QK_NOPE = 64
QK_ROPE = 32
V_HEAD = MAIN_WIDTH // MLA_HEADS
Q_LORA = (3 * D_MODEL) // 8
KV_LORA = D_MODEL // 4
ROPE_THETA = 10000.0
FNET_GROUPS = 4
FNET_GROUP_DIM = MAIN_WIDTH // FNET_GROUPS
EPS = 1e-6

LANES = 128
HEAD_PAD = LANES
QK_PAD = MLA_HEADS * HEAD_PAD
QK_DIM = QK_NOPE + QK_ROPE
ROPE_LO = QK_NOPE
ROPE_HALF = QK_ROPE // 2
GROUP_PAD = 256

FFT_R = 16
FFT_G = FFT_R * FFT_R
FFT_CW = 256
FFT_UNROLL = 16

TM = 1024
TQ = 512
TKV = 256
TM_L1 = 1024
TM_POST = 1024
POST_SUBTILES = 4

VMEM_LIMIT = 56 * 1024 * 1024


def _rmsnorm(x, g):
    return x * lax.rsqrt(jnp.mean(x * x, axis=-1, keepdims=True) + EPS) * g


def _params(n_grid_dims):
    return pltpu.CompilerParams(
        dimension_semantics=("arbitrary",) * n_grid_dims,
        vmem_limit_bytes=VMEM_LIMIT,
    )


def _rope_table_kernel(pos_ref, invf_ref, cos_ref, sin_ref):
    ang = invf_ref[...] * pos_ref[0].astype(F32)
    c, s = jnp.cos(ang), jnp.sin(ang)
    cos_ref[0] = jnp.concatenate([c, c], axis=0)
    sin_ref[0] = jnp.concatenate([-s, s], axis=0)


def _rope_tables(positions):
    b, s = positions.shape
    inv_freq = 1.0 / (ROPE_THETA ** (jnp.arange(0, QK_ROPE, 2, dtype=F32) / QK_ROPE))
    invf = inv_freq.reshape(ROPE_HALF, 1)
    return pl.pallas_call(
        _rope_table_kernel,
        grid=(b,),
        in_specs=[pl.BlockSpec((1, 1, s), lambda i: (i, 0, 0)),
                  pl.BlockSpec((ROPE_HALF, 1), lambda i: (0, 0))],
        out_specs=[pl.BlockSpec((1, QK_ROPE, s), lambda i: (i, 0, 0))] * 2,
        out_shape=[jax.ShapeDtypeStruct((b, QK_ROPE, s), F32)] * 2,
        compiler_params=_params(1),
        name="rope_tables",
    )(positions.reshape(b, 1, s), invf)


def _mem_kv_kernel(mem_ref, g_ref, w_ref, kbd_ref, vbd_ref):
    mn = _rmsnorm(mem_ref[0], g_ref[0]).astype(BF16)
    kv = jnp.dot(mn, w_ref[0], preferred_element_type=F32)
    k_t = (kv[:, :MEM_WIDTH] * (1.0 / float(np.sqrt(MEM_HEAD_DIM)))).T
    v = kv[:, MEM_WIDTH:]
    k_rep = jnp.concatenate([k_t] * MEM_HEADS, axis=1)
    r = lax.broadcasted_iota(jnp.int32, k_rep.shape, 0) // MEM_HEAD_DIM
    c = lax.broadcasted_iota(jnp.int32, k_rep.shape, 1) // N_MEM
    kbd_ref[0, 0] = jnp.where(r == c, k_rep, 0.0).astype(BF16)
    v_rep = jnp.concatenate([v] * MEM_HEADS, axis=0)
    r = lax.broadcasted_iota(jnp.int32, v_rep.shape, 0) // N_MEM
    c = lax.broadcasted_iota(jnp.int32, v_rep.shape, 1) // MEM_HEAD_DIM
    vbd_ref[0, 0] = jnp.where(r == c, v_rep, 0.0).astype(BF16)


def _mem_kv(mem, gains, weights):
    b = mem.shape[0]
    n_layers = gains.shape[0]
    per_layer = lambda i, l: (l, 0, 0)
    out = lambda i, l: (l, i, 0, 0)
    return pl.pallas_call(
        _mem_kv_kernel,
        grid=(b, n_layers),
        in_specs=[pl.BlockSpec((1, N_MEM, D_MODEL), lambda i, l: (i, 0, 0)),
                  pl.BlockSpec((1, 1, D_MODEL), per_layer),
                  pl.BlockSpec((1, D_MODEL, 2 * MEM_WIDTH), per_layer)],
        out_specs=[pl.BlockSpec((1, 1, MEM_WIDTH, MEM_HEADS * N_MEM), out),
                   pl.BlockSpec((1, 1, MEM_HEADS * N_MEM, MEM_WIDTH), out)],
        out_shape=[jax.ShapeDtypeStruct((n_layers, b, MEM_WIDTH, MEM_HEADS * N_MEM), BF16),
                   jax.ShapeDtypeStruct((n_layers, b, MEM_HEADS * N_MEM, MEM_WIDTH), BF16)],
        compiler_params=_params(2),
        name="mem_kv",
    )(mem, gains.reshape(n_layers, 1, D_MODEL), weights.astype(BF16))


_L0_CQ = (0, Q_LORA)
_L0_CKV = (_L0_CQ[1], _L0_CQ[1] + KV_LORA)
_L0_QM = (_L0_CKV[1], _L0_CKV[1] + MEM_WIDTH)
_L0_GATE = (_L0_QM[1], _L0_QM[1] + MIX_WIDTH)
_L0_KR = (_L0_GATE[1], _L0_GATE[1] + HEAD_PAD)
_L0_IN = _L0_KR[1]


def _l0_pre_kernel(x_ref, cos_ref, sin_ref, g_ref, w_in_ref, gq_ref, gkv_ref, wqt_ref, wk_ref,
                   wvt_ref, qt_ref, k_ref, vt_ref, qm_ref, gate_ref):
    h = _rmsnorm(x_ref[0], g_ref[...]).astype(BF16)
    proj = jnp.dot(h, w_in_ref[...], preferred_element_type=F32)
    qm_ref[0] = proj[:, _L0_QM[0]:_L0_QM[1]].astype(BF16)
    gate_ref[0] = proj[:, _L0_GATE[0]:_L0_GATE[1]].astype(BF16)
    cqn = _rmsnorm(proj[:, _L0_CQ[0]:_L0_CQ[1]], gq_ref[...]).astype(BF16)
    ckvn = _rmsnorm(proj[:, _L0_CKV[0]:_L0_CKV[1]], gkv_ref[...]).astype(BF16)
    nt = (((1,), (1,)), ((), ()))
    qt = lax.dot_general(wqt_ref[...], cqn, nt, preferred_element_type=F32)
    kn = jnp.dot(ckvn, wk_ref[...], preferred_element_type=F32)
    vt = lax.dot_general(wvt_ref[...], ckvn, nt, preferred_element_type=F32)
    vt_ref[0] = vt.astype(BF16)

    cos_t = cos_ref[0]
    sin_t = sin_ref[0]

    def rope_t(blk):
        partner = jnp.concatenate([blk[ROPE_HALF:], blk[:ROPE_HALF]], axis=0)
        return blk * cos_t + partner * sin_t

    rope_rows = slice(ROPE_LO, ROPE_LO + QK_ROPE)
    kr_t = proj[:, _L0_KR[0]:_L0_KR[1]].T
    k_rope = jnp.concatenate([kr_t[:ROPE_LO], rope_t(kr_t[rope_rows]), kr_t[ROPE_LO + QK_ROPE:]],
                             axis=0).T
    scale = float(np.log2(np.e)) / float(np.sqrt(QK_NOPE + QK_ROPE))
    nope_lane = lax.broadcasted_iota(jnp.int32, k_rope.shape, 1) < QK_NOPE
    for hd in range(MLA_HEADS):
        sl = slice(hd * HEAD_PAD, (hd + 1) * HEAD_PAD)
        rows = slice(hd * QK_DIM, (hd + 1) * QK_DIM)
        q_h = qt[rows]
        q_h = jnp.concatenate([q_h[:ROPE_LO], rope_t(q_h[rope_rows])], axis=0)
        qt_ref[0, rows, :] = (q_h * scale).astype(BF16)
        pair = kn[:, (hd // 2) * HEAD_PAD:(hd // 2 + 1) * HEAD_PAD]
        if hd % 2:
            pair = pltpu.roll(pair, QK_NOPE, 1)
        k_ref[0, :, sl] = jnp.where(nope_lane, pair, k_rope).astype(BF16)


def _l0_pre(x, cos_t, sin_t, norm_g, w_in, q_norm_g, kv_norm_g, w_uq, w_ukv):
    b, s, _ = x.shape
    o1, o2, o3, o4 = Q_LORA, Q_LORA + KV_LORA, Q_LORA + KV_LORA + QK_ROPE, Q_LORA + KV_LORA + QK_ROPE + MEM_WIDTH
    w_kr = jnp.pad(w_in[:, o2:o3], ((0, 0), (ROPE_LO, HEAD_PAD - ROPE_LO - QK_ROPE)))
    w_in_p = jnp.concatenate([w_in[:, :o2], w_in[:, o3:o4], w_in[:, o4:], w_kr], axis=1).astype(BF16)
    wqt = w_uq.T.astype(BF16)
    w_ukv3 = w_ukv.reshape(KV_LORA, MLA_HEADS, QK_NOPE + V_HEAD)
    wk = w_ukv3[:, :, :QK_NOPE].reshape(KV_LORA, MLA_HEADS * QK_NOPE).astype(BF16)
    wvt = w_ukv3[:, :, QK_NOPE:].reshape(KV_LORA, MAIN_WIDTH).T.astype(BF16)

    row = lambda i, j: (i, j, 0)
    col = lambda i, j: (i, 0, j)
    const = lambda i, j: (0, 0)
    return pl.pallas_call(
        _l0_pre_kernel,
        grid=(b, s // TM),
        in_specs=[pl.BlockSpec((1, TM, D_MODEL), row),
                  pl.BlockSpec((1, QK_ROPE, TM), col),
                  pl.BlockSpec((1, QK_ROPE, TM), col),
                  pl.BlockSpec((1, D_MODEL), const),
                  pl.BlockSpec((D_MODEL, _L0_IN), const, pipeline_mode=pl.Buffered(1)),
                  pl.BlockSpec((1, Q_LORA), const),
                  pl.BlockSpec((1, KV_LORA), const),
                  pl.BlockSpec((MLA_HEADS * QK_DIM, Q_LORA), const, pipeline_mode=pl.Buffered(1)),
                  pl.BlockSpec((KV_LORA, MLA_HEADS * QK_NOPE), const, pipeline_mode=pl.Buffered(1)),
                  pl.BlockSpec((MAIN_WIDTH, KV_LORA), const, pipeline_mode=pl.Buffered(1))],
        out_specs=[pl.BlockSpec((1, MLA_HEADS * QK_DIM, TM), col),
                   pl.BlockSpec((1, TM, QK_PAD), row),
                   pl.BlockSpec((1, MAIN_WIDTH, TM), col),
                   pl.BlockSpec((1, TM, MEM_WIDTH), row),
                   pl.BlockSpec((1, TM, MIX_WIDTH), row)],
        out_shape=[jax.ShapeDtypeStruct((b, MLA_HEADS * QK_DIM, s), BF16),
                   jax.ShapeDtypeStruct((b, s, QK_PAD), BF16),
                   jax.ShapeDtypeStruct((b, MAIN_WIDTH, s), BF16),
                   jax.ShapeDtypeStruct((b, s, MEM_WIDTH), BF16),
                   jax.ShapeDtypeStruct((b, s, MIX_WIDTH), BF16)],
        compiler_params=_params(2),
        name="l0_pre",
    )(x, cos_t, sin_t, norm_g.reshape(1, -1), w_in_p, q_norm_g.reshape(1, -1),
      kv_norm_g.reshape(1, -1), wqt, wk, wvt)


HEADS_PER_STEP = 2
SUM_ROWS = 16
FINISH_AFTER_CHUNK = 1
SCORE_AHEAD = 1
SCORE_SLOTS = 2


def _attn_kernel(qt_ref, k_ref, vt_ref, o_ref, s_ref, mx_ref, m_ref, acc_ref):
    n_tiles = k_ref.shape[1] // TQ
    n_chunks = k_ref.shape[1] // TKV
    assert n_chunks % SCORE_SLOTS == 0

    def scores(hd, tile, chunk, slot):
        q_off = pl.multiple_of(tile * TQ, TQ)
        k_off = pl.multiple_of(chunk * TKV, TKV)
        lanes = slice(hd * HEAD_PAD, (hd + 1) * HEAD_PAD)
        q_t = jnp.concatenate([qt_ref[0, hd * QK_DIM:(hd + 1) * QK_DIM, pl.ds(q_off, TQ)],
                               jnp.zeros((HEAD_PAD - QK_DIM, TQ), BF16)], axis=0)
        s_t = jnp.dot(k_ref[0, pl.ds(k_off, TKV), lanes], q_t, preferred_element_type=F32)
        s_ref[slot, hd] = s_t
        mx_ref[slot, hd] = jnp.max(s_t, axis=0, keepdims=True)

    def consume(hd, chunk, slot, par):
        k_off = pl.multiple_of(chunk * TKV, TKV)
        m_old = m_ref[hd]
        m_new = jnp.maximum(m_old, mx_ref[slot, hd])
        alpha = jnp.exp2(m_old - m_new)
        p = jnp.exp2(s_ref[slot, hd] - m_new).astype(BF16)
        m_ref[hd] = m_new
        v_aug = jnp.concatenate([vt_ref[0, hd * V_HEAD:(hd + 1) * V_HEAD, pl.ds(k_off, TKV)],
                                 jnp.ones((SUM_ROWS, TKV), BF16)], axis=0)
        acc_ref[par, hd] = alpha * acc_ref[par, hd] + jnp.dot(v_aug, p, preferred_element_type=F32)

    def finish(t, par):
        out = [acc_ref[par, hd, :V_HEAD, :] * (1.0 / acc_ref[par, hd, V_HEAD:V_HEAD + 1, :])
               for hd in range(HEADS_PER_STEP)]
        o_ref[0, pl.ds(pl.multiple_of(t * TQ, TQ), TQ), :] = jnp.concatenate(out, axis=0).T.astype(BF16)

    def tile(t, carry):
        par = t % 2
        m_ref[...] = jnp.full(m_ref.shape, -jnp.inf, F32)
        acc_ref[par] = jnp.zeros(acc_ref.shape[1:], F32)
        for c in range(n_chunks):
            ahead = c + SCORE_AHEAD
            nxt = (t, ahead) if ahead < n_chunks else (jnp.minimum(t + 1, n_tiles - 1), ahead - n_chunks)
            for hd in range(HEADS_PER_STEP):
                scores(hd, *nxt, ahead % SCORE_SLOTS)
                consume(hd, c, c % SCORE_SLOTS, par)
            if c == FINISH_AFTER_CHUNK:
                finish(jnp.maximum(t - 1, 0), 1 - par)
        return carry

    for c in range(SCORE_AHEAD):
        for hd in range(HEADS_PER_STEP):
            scores(hd, 0, c, c)
    acc_ref[...] = jnp.ones(acc_ref.shape, F32)
    lax.fori_loop(0, n_tiles, tile, 0)
    finish(n_tiles - 1, (n_tiles - 1) % 2)


def _attention(qt, k, vt):
    b, s, _ = k.shape
    width = HEADS_PER_STEP * HEAD_PAD
    vrows = HEADS_PER_STEP * V_HEAD
    return pl.pallas_call(
        _attn_kernel,
        grid=(b, MLA_HEADS // HEADS_PER_STEP),
        in_specs=[pl.BlockSpec((1, HEADS_PER_STEP * QK_DIM, s), lambda i, j: (i, j, 0)),
                  pl.BlockSpec((1, s, width), lambda i, j: (i, 0, j)),
                  pl.BlockSpec((1, vrows, s), lambda i, j: (i, j, 0))],
        out_specs=pl.BlockSpec((1, s, vrows), lambda i, j: (i, 0, j)),
        out_shape=jax.ShapeDtypeStruct((b, s, MAIN_WIDTH), BF16),
        scratch_shapes=[pltpu.VMEM((SCORE_SLOTS, HEADS_PER_STEP, TKV, TQ), F32),
                        pltpu.VMEM((SCORE_SLOTS, HEADS_PER_STEP, 1, TQ), F32),
                        pltpu.VMEM((HEADS_PER_STEP, 1, TQ), F32),
                        pltpu.VMEM((2, HEADS_PER_STEP, V_HEAD + SUM_ROWS, TQ), F32)],
        compiler_params=_params(2),
        name="mla_attention",
    )(qt, k, vt)


def _post_kernel(*refs, final_norm):
    if final_norm:
        x_ref, main_ref, qm_ref, gate_ref, kbd_ref, vbd_ref, wout_ref, fg_ref, o_ref = refs
    else:
        x_ref, main_ref, qm_ref, gate_ref, kbd_ref, vbd_ref, wout_ref, o_ref = refs
    sub = x_ref.shape[1] // POST_SUBTILES
    rows = [slice(p * sub, (p + 1) * sub) for p in range(POST_SUBTILES)]

    def scores(p, _):
        return jnp.dot(qm_ref[0, rows[p]], kbd_ref[0, 0], preferred_element_type=F32)

    def softmax(p, s):
        probs = []
        for hd in range(MEM_HEADS):
            sh = s[:, hd * N_MEM:(hd + 1) * N_MEM]
            e = jnp.exp(sh - jnp.max(sh, axis=-1, keepdims=True))
            probs.append((e * (1.0 / jnp.sum(e, axis=-1, keepdims=True))).astype(BF16))
        return jnp.concatenate(probs, axis=-1)

    def values(p, probs):
        return jnp.dot(probs, vbd_ref[0, 0], preferred_element_type=F32)

    def gating(p, mem_out):
        gate = gate_ref[0, rows[p]].astype(F32)
        branch = jnp.concatenate([main_ref[0, rows[p]].astype(F32), mem_out], axis=-1)
        return (branch * (0.5 * gate * (1.0 + jnp.tanh(0.5 * gate)))).astype(BF16)

    def project(p, branch):
        y = x_ref[0, rows[p]] + jnp.dot(branch, wout_ref[...], preferred_element_type=F32)
        if final_norm:
            y = _rmsnorm(y, fg_ref[...])
        o_ref[0, rows[p]] = y

    stages = (scores, softmax, values, gating, project)
    state = [None] * POST_SUBTILES
    for step in range(len(stages) + POST_SUBTILES - 1):
        for p in range(POST_SUBTILES):
            if 0 <= step - p < len(stages):
                state[p] = stages[step - p](p, state[p])


def _post(x, main, qm, gate, kbd, vbd, layer, w_out, final_g=None):
    b, s, _ = x.shape
    row = lambda i, j: (i, j, 0)
    per_batch = lambda i, j: (layer, i, 0, 0)
    const = lambda i, j: (0, 0)
    in_specs = [pl.BlockSpec((1, TM_POST, D_MODEL), row),
                pl.BlockSpec((1, TM_POST, MAIN_WIDTH), row),
                pl.BlockSpec((1, TM_POST, MEM_WIDTH), row),
                pl.BlockSpec((1, TM_POST, MIX_WIDTH), row),
                pl.BlockSpec((1, 1, MEM_WIDTH, MEM_HEADS * N_MEM), per_batch),
                pl.BlockSpec((1, 1, MEM_HEADS * N_MEM, MEM_WIDTH), per_batch),
                pl.BlockSpec((MIX_WIDTH, D_MODEL), const)]
    args = [x, main, qm, gate, kbd, vbd, w_out.astype(BF16)]
    if final_g is not None:
        in_specs.append(pl.BlockSpec((1, D_MODEL), const))
        args.append(final_g.reshape(1, -1))
    return pl.pallas_call(
        functools.partial(_post_kernel, final_norm=final_g is not None),
        grid=(b, s // TM_POST),
        in_specs=in_specs,
        out_specs=pl.BlockSpec((1, TM_POST, D_MODEL), row),
        out_shape=jax.ShapeDtypeStruct((b, s, D_MODEL), F32),
        compiler_params=_params(2),
        name="post_final" if final_g is not None else "post",
    )(*args)


def _fold_kernel(wf_ref, wn_ref, cc_ref, sc_ref, oa_ref, ob_ref):
    hp = lax.Precision.HIGHEST
    a = jnp.dot(cc_ref[...], wn_ref[0], precision=hp, preferred_element_type=F32)
    bm = jnp.dot(sc_ref[...], wn_ref[0], precision=hp, preferred_element_type=F32)
    oa_ref[0] = jnp.dot(wf_ref[0], a, precision=hp, preferred_element_type=F32)
    ob_ref[0] = jnp.dot(wf_ref[0], bm, precision=hp, preferred_element_type=F32)


def _channel_dft_tables(seq_len):
    n = FNET_GROUP_DIM
    jk = np.outer(np.arange(n), np.arange(n)) % n
    ang = 2.0 * np.pi * jk / n
    alpha = 1.0 / np.sqrt(float(seq_len) * n)
    cc = np.zeros((GROUP_PAD, GROUP_PAD), np.float32)
    sc = np.zeros((GROUP_PAD, GROUP_PAD), np.float32)
    cc[:n, :n] = alpha * np.cos(ang)
    sc[:n, :n] = -alpha * np.sin(ang)
    return jnp.asarray(cc), jnp.asarray(sc)


def _fold_fnet(w_in_f, w_fnet, seq_len):
    gpad = GROUP_PAD - FNET_GROUP_DIM
    wf = jnp.pad(w_in_f.reshape(D_MODEL, FNET_GROUPS, FNET_GROUP_DIM).transpose(1, 0, 2),
                 ((0, 0), (0, 0), (0, gpad)))
    wn = jnp.pad(w_fnet, ((0, 0), (0, gpad), (0, gpad)))
    cc, sc = _channel_dft_tables(seq_len)
    grp = lambda g: (g, 0, 0)
    const = lambda g: (0, 0)
    oa, ob = pl.pallas_call(
        _fold_kernel,
        grid=(FNET_GROUPS,),
        in_specs=[pl.BlockSpec((1, D_MODEL, GROUP_PAD), grp),
                  pl.BlockSpec((1, GROUP_PAD, GROUP_PAD), grp),
                  pl.BlockSpec((GROUP_PAD, GROUP_PAD), const),
                  pl.BlockSpec((GROUP_PAD, GROUP_PAD), const)],
        out_specs=[pl.BlockSpec((1, D_MODEL, GROUP_PAD), grp)] * 2,
        out_shape=[jax.ShapeDtypeStruct((FNET_GROUPS, D_MODEL, GROUP_PAD), F32)] * 2,
        compiler_params=_params(1),
        name="fnet_fold",
    )(wf, wn, cc, sc)
    unpack = lambda o: o[:, :, :FNET_GROUP_DIM].transpose(1, 0, 2).reshape(D_MODEL, MAIN_WIDTH)
    return unpack(oa), unpack(ob)


_L1_IN = 2 * MAIN_WIDTH + MEM_WIDTH + MIX_WIDTH


def _l1_pre_kernel(x_ref, g_ref, w_ref, u_ref, v_ref, qm_ref, gate_ref):
    h = _rmsnorm(x_ref[0], g_ref[...]).astype(BF16)
    proj = jnp.dot(h, w_ref[...], preferred_element_type=F32)
    u_ref[0] = proj[:, :MAIN_WIDTH].astype(BF16)
    v_ref[0] = proj[:, MAIN_WIDTH:2 * MAIN_WIDTH].astype(BF16)
    qm_ref[0] = proj[:, 2 * MAIN_WIDTH:2 * MAIN_WIDTH + MEM_WIDTH].astype(BF16)
    gate_ref[0] = proj[:, 2 * MAIN_WIDTH + MEM_WIDTH:].astype(BF16)


def _l1_pre(x, norm_g, w_all):
    b, s, _ = x.shape
    row = lambda i, j: (i, j, 0)
    const = lambda i, j: (0, 0)
    return pl.pallas_call(
        _l1_pre_kernel,
        grid=(b, s // TM_L1),
        in_specs=[pl.BlockSpec((1, TM_L1, D_MODEL), row),
                  pl.BlockSpec((1, D_MODEL), const),
                  pl.BlockSpec((D_MODEL, _L1_IN), const, pipeline_mode=pl.Buffered(1))],
        out_specs=[pl.BlockSpec((1, TM_L1, MAIN_WIDTH), row),
                   pl.BlockSpec((1, TM_L1, MAIN_WIDTH), row),
                   pl.BlockSpec((1, TM_L1, MEM_WIDTH), row),
                   pl.BlockSpec((1, TM_L1, MIX_WIDTH), row)],
        out_shape=[jax.ShapeDtypeStruct((b, s, MAIN_WIDTH), BF16),
                   jax.ShapeDtypeStruct((b, s, MAIN_WIDTH), BF16),
                   jax.ShapeDtypeStruct((b, s, MEM_WIDTH), BF16),
                   jax.ShapeDtypeStruct((b, s, MIX_WIDTH), BF16)],
        compiler_params=_params(2),
        name="l1_pre",
    )(x, norm_g.reshape(1, -1), w_all)


def _dft3_tables():
    r = FFT_R
    i = np.arange(r)
    w = lambda m, e: np.exp(-2j * np.pi * (e % m) / m)
    f16 = w(r, np.outer(i, i))
    eye = np.eye(r)
    ga = np.einsum("ed,ka,bk->bekad", eye, f16, w(r * r, np.outer(i, i))).reshape(r, FFT_G, FFT_G)
    tw = w(r ** 3, i[:, None, None] * (i[None, None, :] + r * i[None, :, None]))
    gb = np.einsum("jk,qb,dqj->dqjbk", eye, f16, tw).reshape(r, FFT_G, FFT_G)
    gc = np.einsum("jk,qd->qjdk", eye, f16).reshape(FFT_G, FFT_G)
    stack = lambda g: np.concatenate([np.concatenate([g.real, -g.imag], axis=-1),
                                      np.concatenate([g.imag, g.real], axis=-1)], axis=-2)
    real_rows = np.concatenate([gc.real, -gc.imag], axis=-1)
    f32 = lambda a: jnp.asarray(a.astype(np.float32))
    return f32(stack(ga)), f32(stack(gb)), f32(real_rows)


def _dft3_kernel(u_ref, v_ref, ga_ref, gb_ref, gc_ref, o_ref, yr_ref, yi_ref):
    r, g = FFT_R, FFT_G
    tiles = (r, r, FFT_CW)

    def apply(mat, zr, zi):
        z = jnp.concatenate([zr.reshape(g, FFT_CW), zi.reshape(g, FFT_CW)], axis=0)
        return jnp.dot(mat, z, preferred_element_type=F32)

    def stage_a(b, carry):
        y = apply(ga_ref[b], u_ref[0, :, b], v_ref[0, :, b])
        yr_ref[:, b] = y[:g].astype(BF16).reshape(tiles)
        yi_ref[:, b] = y[g:].astype(BF16).reshape(tiles)
        return carry

    def stage_b(d, carry):
        y = apply(gb_ref[d], yr_ref[d], yi_ref[d])
        yr_ref[d] = y[:g].astype(BF16).reshape(tiles)
        yi_ref[d] = y[g:].astype(BF16).reshape(tiles)
        return carry

    def stage_c(kb, carry):
        y = apply(gc_ref[...], yr_ref[:, kb], yi_ref[:, kb])
        o_ref[0, :, kb] = y.astype(BF16).reshape(tiles)
        return carry

    lax.fori_loop(0, r, stage_a, 0, unroll=FFT_UNROLL)
    lax.fori_loop(0, r, stage_b, 0, unroll=FFT_UNROLL)
    lax.fori_loop(0, r, stage_c, 0, unroll=FFT_UNROLL)


def _seq_dft_real(u, v):
    b, s, w = u.shape
    r = FFT_R
    assert s == r ** 3 and w % FFT_CW == 0
    ga, gb, gc = (t.astype(BF16) for t in _dft3_tables())
    shape5 = (b, r, r, r, w)
    blk = pl.BlockSpec((1, r, r, r, FFT_CW), lambda i, c: (i, 0, 0, 0, c))
    out = pl.pallas_call(
        _dft3_kernel,
        grid=(b, w // FFT_CW),
        in_specs=[blk, blk,
                  pl.BlockSpec((r, 2 * FFT_G, 2 * FFT_G), lambda i, c: (0, 0, 0), pipeline_mode=pl.Buffered(1)),
                  pl.BlockSpec((r, 2 * FFT_G, 2 * FFT_G), lambda i, c: (0, 0, 0), pipeline_mode=pl.Buffered(1)),
                  pl.BlockSpec((FFT_G, 2 * FFT_G), lambda i, c: (0, 0), pipeline_mode=pl.Buffered(1))],
        out_specs=blk,
        out_shape=jax.ShapeDtypeStruct(shape5, BF16),
        scratch_shapes=[pltpu.VMEM((r, r, r, FFT_CW), BF16), pltpu.VMEM((r, r, r, FFT_CW), BF16)],
        compiler_params=_params(2),
        name="seq_dft",
    )(u.reshape(shape5), v.reshape(shape5), ga, gb, gc)
    return out.reshape(b, s, w)


def kernel(x, mem, positions, norm_g_l0, w_in_l0, q_norm_g_l0, kv_norm_g_l0, w_uq_l0, w_ukv_l0,
           mem_norm_g_l0, w_mem_kv_l0, w_out_l0, norm_g_l1, w_in_l1, w_fnet_l1, mem_norm_g_l1,
           w_mem_kv_l1, w_out_l1, final_norm_g):
    s = x.shape[1]
    cos_t, sin_t = _rope_tables(positions)
    kbd, vbd = _mem_kv(mem, jnp.stack([mem_norm_g_l0, mem_norm_g_l1]),
                       jnp.stack([w_mem_kv_l0, w_mem_kv_l1]))

    qt, k, vt, qm0, gate0 = _l0_pre(x, cos_t, sin_t, norm_g_l0, w_in_l0, q_norm_g_l0, kv_norm_g_l0,
                                    w_uq_l0, w_ukv_l0)
    attn = _attention(qt, k, vt)
    x1 = _post(x, attn, qm0, gate0, kbd, vbd, 0, w_out_l0)

    w_a, w_b = _fold_fnet(w_in_l1[:, :MAIN_WIDTH], w_fnet_l1, s)
    w_all = jnp.concatenate([w_a, w_b, w_in_l1[:, MAIN_WIDTH:]], axis=1).astype(BF16)
    u, v, qm1, gate1 = _l1_pre(x1, norm_g_l1, w_all)
    mixed = _seq_dft_real(u, v)
    return _post(x1, mixed, qm1, gate1, kbd, vbd, 1, w_out_l1, final_g=final_norm_g)
```

```python
import functools

import numpy as np
import jax
import jax.numpy as jnp
from jax import lax
from jax.experimental import pallas as pl
from jax.experimental.pallas import tpu as pltpu

F32 = jnp.float32
BF16 = jnp.bfloat16

D_MODEL = 1024
N_MEM = 256
MIX_WIDTH = D_MODEL
MEM_HEADS = 4
MEM_WIDTH = MIX_WIDTH // 4
MEM_HEAD_DIM = MEM_WIDTH // MEM_HEADS
MAIN_WIDTH = MIX_WIDTH - MEM_WIDTH
MLA_HEADS = 12
QK_NOPE = 64
QK_ROPE = 32
V_HEAD = MAIN_WIDTH // MLA_HEADS
Q_LORA = (3 * D_MODEL) // 8
KV_LORA = D_MODEL // 4
ROPE_THETA = 10000.0
FNET_GROUPS = 4
FNET_GROUP_DIM = MAIN_WIDTH // FNET_GROUPS
EPS = 1e-6

LANES = 128
HEAD_PAD = LANES
QK_PAD = MLA_HEADS * HEAD_PAD
QK_DIM = QK_NOPE + QK_ROPE
ROPE_LO = QK_NOPE
ROPE_HALF = QK_ROPE // 2
GROUP_PAD = 256

FFT_R = 16
FFT_G = FFT_R * FFT_R
FFT_CW = 256
FFT_UNROLL = 16

TM = 1024
TQ = 512
TKV = 256
TM_L1 = 1024
TM_POST = 1024
POST_SUBTILES = 4

VMEM_LIMIT = 56 * 1024 * 1024


def _rmsnorm(x, g):
    return x * lax.rsqrt(jnp.mean(x * x, axis=-1, keepdims=True) + EPS) * g


def _params(n_grid_dims):
    return pltpu.CompilerParams(
        dimension_semantics=("arbitrary",) * n_grid_dims,
        vmem_limit_bytes=VMEM_LIMIT,
    )


def _rope_table_kernel(pos_ref, invf_ref, cos_ref, sin_ref):
    ang = invf_ref[...] * pos_ref[0].astype(F32)
    c, s = jnp.cos(ang), jnp.sin(ang)
    cos_ref[0] = jnp.concatenate([c, c], axis=0)
    sin_ref[0] = jnp.concatenate([-s, s], axis=0)


def _rope_tables(positions):
    b, s = positions.shape
    inv_freq = 1.0 / (ROPE_THETA ** (jnp.arange(0, QK_ROPE, 2, dtype=F32) / QK_ROPE))
    invf = inv_freq.reshape(ROPE_HALF, 1)
    return pl.pallas_call(
        _rope_table_kernel,
        grid=(b,),
        in_specs=[pl.BlockSpec((1, 1, s), lambda i: (i, 0, 0)),
                  pl.BlockSpec((ROPE_HALF, 1), lambda i: (0, 0))],
        out_specs=[pl.BlockSpec((1, QK_ROPE, s), lambda i: (i, 0, 0))] * 2,
        out_shape=[jax.ShapeDtypeStruct((b, QK_ROPE, s), F32)] * 2,
        compiler_params=_params(1),
        name="rope_tables",
    )(positions.reshape(b, 1, s), invf)


def _mem_kv_kernel(mem_ref, g_ref, w_ref, kbd_ref, vbd_ref):
    mn = _rmsnorm(mem_ref[0], g_ref[0]).astype(BF16)
    kv = jnp.dot(mn, w_ref[0], preferred_element_type=F32)
    k_t = (kv[:, :MEM_WIDTH] * (1.0 / float(np.sqrt(MEM_HEAD_DIM)))).T
    v = kv[:, MEM_WIDTH:]
    k_rep = jnp.concatenate([k_t] * MEM_HEADS, axis=1)
    r = lax.broadcasted_iota(jnp.int32, k_rep.shape, 0) // MEM_HEAD_DIM
    c = lax.broadcasted_iota(jnp.int32, k_rep.shape, 1) // N_MEM
    kbd_ref[0, 0] = jnp.where(r == c, k_rep, 0.0).astype(BF16)
    v_rep = jnp.concatenate([v] * MEM_HEADS, axis=0)
    r = lax.broadcasted_iota(jnp.int32, v_rep.shape, 0) // N_MEM
    c = lax.broadcasted_iota(jnp.int32, v_rep.shape, 1) // MEM_HEAD_DIM
    vbd_ref[0, 0] = jnp.where(r == c, v_rep, 0.0).astype(BF16)


def _mem_kv(mem, gains, weights):
    b = mem.shape[0]
    n_layers = gains.shape[0]
    per_layer = lambda i, l: (l, 0, 0)
    out = lambda i, l: (l, i, 0, 0)
    return pl.pallas_call(
        _mem_kv_kernel,
        grid=(b, n_layers),
        in_specs=[pl.BlockSpec((1, N_MEM, D_MODEL), lambda i, l: (i, 0, 0)),
                  pl.BlockSpec((1, 1, D_MODEL), per_layer),
                  pl.BlockSpec((1, D_MODEL, 2 * MEM_WIDTH), per_layer)],
        out_specs=[pl.BlockSpec((1, 1, MEM_WIDTH, MEM_HEADS * N_MEM), out),
                   pl.BlockSpec((1, 1, MEM_HEADS * N_MEM, MEM_WIDTH), out)],
        out_shape=[jax.ShapeDtypeStruct((n_layers, b, MEM_WIDTH, MEM_HEADS * N_MEM), BF16),
                   jax.ShapeDtypeStruct((n_layers, b, MEM_HEADS * N_MEM, MEM_WIDTH), BF16)],
        compiler_params=_params(2),
        name="mem_kv",
    )(mem, gains.reshape(n_layers, 1, D_MODEL), weights.astype(BF16))


_L0_CQ = (0, Q_LORA)
_L0_CKV = (_L0_CQ[1], _L0_CQ[1] + KV_LORA)
_L0_QM = (_L0_CKV[1], _L0_CKV[1] + MEM_WIDTH)
_L0_GATE = (_L0_QM[1], _L0_QM[1] + MIX_WIDTH)
_L0_KR = (_L0_GATE[1], _L0_GATE[1] + HEAD_PAD)
_L0_IN = _L0_KR[1]


def _l0_pre_kernel(x_ref, cos_ref, sin_ref, g_ref, w_in_ref, gq_ref, gkv_ref, wqt_ref, wk_ref,
                   wvt_ref, qt_ref, k_ref, vt_ref, qm_ref, gate_ref):
    h = _rmsnorm(x_ref[0], g_ref[...]).astype(BF16)
    proj = jnp.dot(h, w_in_ref[...], preferred_element_type=F32)
    qm_ref[0] = proj[:, _L0_QM[0]:_L0_QM[1]].astype(BF16)
    gate_ref[0] = proj[:, _L0_GATE[0]:_L0_GATE[1]].astype(BF16)
    cqn = _rmsnorm(proj[:, _L0_CQ[0]:_L0_CQ[1]], gq_ref[...]).astype(BF16)
    ckvn = _rmsnorm(proj[:, _L0_CKV[0]:_L0_CKV[1]], gkv_ref[...]).astype(BF16)
    nt = (((1,), (1,)), ((), ()))
    qt = lax.dot_general(wqt_ref[...], cqn, nt, preferred_element_type=F32)
    kn = jnp.dot(ckvn, wk_ref[...], preferred_element_type=F32)
    vt = lax.dot_general(wvt_ref[...], ckvn, nt, preferred_element_type=F32)
    vt_ref[0] = vt.astype(BF16)

    cos_t = cos_ref[0]
    sin_t = sin_ref[0]

    def rope_t(blk):
        partner = jnp.concatenate([blk[ROPE_HALF:], blk[:ROPE_HALF]], axis=0)
        return blk * cos_t + partner * sin_t

    rope_rows = slice(ROPE_LO, ROPE_LO + QK_ROPE)
    kr_t = proj[:, _L0_KR[0]:_L0_KR[1]].T
    k_rope = jnp.concatenate([kr_t[:ROPE_LO], rope_t(kr_t[rope_rows]), kr_t[ROPE_LO + QK_ROPE:]],
                             axis=0).T
    scale = float(np.log2(np.e)) / float(np.sqrt(QK_NOPE + QK_ROPE))
    nope_lane = lax.broadcasted_iota(jnp.int32, k_rope.shape, 1) < QK_NOPE
    for hd in range(MLA_HEADS):
        sl = slice(hd * HEAD_PAD, (hd + 1) * HEAD_PAD)
        rows = slice(hd * QK_DIM, (hd + 1) * QK_DIM)
        q_h = qt[rows]
        q_h = jnp.concatenate([q_h[:ROPE_LO], rope_t(q_h[rope_rows])], axis=0)
        qt_ref[0, rows, :] = (q_h * scale).astype(BF16)
        pair = kn[:, (hd // 2) * HEAD_PAD:(hd // 2 + 1) * HEAD_PAD]
        if hd % 2:
            pair = pltpu.roll(pair, QK_NOPE, 1)
        k_ref[0, :, sl] = jnp.where(nope_lane, pair, k_rope).astype(BF16)


def _l0_pre(x, cos_t, sin_t, norm_g, w_in, q_norm_g, kv_norm_g, w_uq, w_ukv):
    b, s, _ = x.shape
    o1, o2, o3, o4 = Q_LORA, Q_LORA + KV_LORA, Q_LORA + KV_LORA + QK_ROPE, Q_LORA + KV_LORA + QK_ROPE + MEM_WIDTH
    w_kr = jnp.pad(w_in[:, o2:o3], ((0, 0), (ROPE_LO, HEAD_PAD - ROPE_LO - QK_ROPE)))
    w_in_p = jnp.concatenate([w_in[:, :o2], w_in[:, o3:o4], w_in[:, o4:], w_kr], axis=1).astype(BF16)
    wqt = w_uq.T.astype(BF16)
    w_ukv3 = w_ukv.reshape(KV_LORA, MLA_HEADS, QK_NOPE + V_HEAD)
    wk = w_ukv3[:, :, :QK_NOPE].reshape(KV_LORA, MLA_HEADS * QK_NOPE).astype(BF16)
    wvt = w_ukv3[:, :, QK_NOPE:].reshape(KV_LORA, MAIN_WIDTH).T.astype(BF16)

    row = lambda i, j: (i, j, 0)
    col = lambda i, j: (i, 0, j)
    const = lambda i, j: (0, 0)
    return pl.pallas_call(
        _l0_pre_kernel,
        grid=(b, s // TM),
        in_specs=[pl.BlockSpec((1, TM, D_MODEL), row),
                  pl.BlockSpec((1, QK_ROPE, TM), col),
                  pl.BlockSpec((1, QK_ROPE, TM), col),
                  pl.BlockSpec((1, D_MODEL), const),
                  pl.BlockSpec((D_MODEL, _L0_IN), const, pipeline_mode=pl.Buffered(1)),
                  pl.BlockSpec((1, Q_LORA), const),
                  pl.BlockSpec((1, KV_LORA), const),
                  pl.BlockSpec((MLA_HEADS * QK_DIM, Q_LORA), const, pipeline_mode=pl.Buffered(1)),
                  pl.BlockSpec((KV_LORA, MLA_HEADS * QK_NOPE), const, pipeline_mode=pl.Buffered(1)),
                  pl.BlockSpec((MAIN_WIDTH, KV_LORA), const, pipeline_mode=pl.Buffered(1))],
        out_specs=[pl.BlockSpec((1, MLA_HEADS * QK_DIM, TM), col),
                   pl.BlockSpec((1, TM, QK_PAD), row),
                   pl.BlockSpec((1, MAIN_WIDTH, TM), col),
                   pl.BlockSpec((1, TM, MEM_WIDTH), row),
                   pl.BlockSpec((1, TM, MIX_WIDTH), row)],
        out_shape=[jax.ShapeDtypeStruct((b, MLA_HEADS * QK_DIM, s), BF16),
                   jax.ShapeDtypeStruct((b, s, QK_PAD), BF16),
                   jax.ShapeDtypeStruct((b, MAIN_WIDTH, s), BF16),
                   jax.ShapeDtypeStruct((b, s, MEM_WIDTH), BF16),
                   jax.ShapeDtypeStruct((b, s, MIX_WIDTH), BF16)],
        compiler_params=_params(2),
        name="l0_pre",
    )(x, cos_t, sin_t, norm_g.reshape(1, -1), w_in_p, q_norm_g.reshape(1, -1),
      kv_norm_g.reshape(1, -1), wqt, wk, wvt)


HEADS_PER_STEP = 2
SUM_ROWS = 16
SCORE_AHEAD = 1
SCORE_SLOTS = 2


def _attn_kernel(qt_ref, k_ref, vt_ref, o_ref, s_ref, mx_ref, m_ref, acc_ref):
    n_tiles = k_ref.shape[1] // TQ
    n_chunks = k_ref.shape[1] // TKV
    assert n_chunks % SCORE_SLOTS == 0

    def scores(hd, tile, chunk, slot):
        q_off = pl.multiple_of(tile * TQ, TQ)
        k_off = pl.multiple_of(chunk * TKV, TKV)
        lanes = slice(hd * HEAD_PAD, (hd + 1) * HEAD_PAD)
        q_t = jnp.concatenate([qt_ref[0, hd * QK_DIM:(hd + 1) * QK_DIM, pl.ds(q_off, TQ)],
                               jnp.zeros((HEAD_PAD - QK_DIM, TQ), BF16)], axis=0)
        s_t = jnp.dot(k_ref[0, pl.ds(k_off, TKV), lanes], q_t, preferred_element_type=F32)
        s_ref[slot, hd] = s_t
        mx_ref[slot, hd] = jnp.max(s_t, axis=0, keepdims=True)

    def consume(hd, chunk, slot, par):
        k_off = pl.multiple_of(chunk * TKV, TKV)
        m_old = m_ref[hd]
        m_new = jnp.maximum(m_old, mx_ref[slot, hd])
        alpha = jnp.exp2(m_old - m_new)
        p = jnp.exp2(s_ref[slot, hd] - m_new).astype(BF16)
        m_ref[hd] = m_new
        v_aug = jnp.concatenate([vt_ref[0, hd * V_HEAD:(hd + 1) * V_HEAD, pl.ds(k_off, TKV)],
                                 jnp.ones((SUM_ROWS, TKV), BF16)], axis=0)
        acc_ref[par, hd] = alpha * acc_ref[par, hd] + jnp.dot(v_aug, p, preferred_element_type=F32)

    def finish(t, par):
        out = [acc_ref[par, hd, :V_HEAD, :] * (1.0 / acc_ref[par, hd, V_HEAD:V_HEAD + 1, :])
               for hd in range(HEADS_PER_STEP)]
        o_ref[0, pl.ds(pl.multiple_of(t * TQ, TQ), TQ), :] = jnp.concatenate(out, axis=0).T.astype(BF16)

    def tile_pair(j, carry):
        for par in range(2):
            t = 2 * j + par
            m_ref[...] = jnp.full(m_ref.shape, -jnp.inf, F32)
            acc_ref[par] = jnp.zeros(acc_ref.shape[1:], F32)
            for c in range(n_chunks):
                ahead = c + SCORE_AHEAD
                nxt = (t, ahead) if ahead < n_chunks else (jnp.minimum(t + 1, n_tiles - 1), ahead - n_chunks)
                for hd in range(HEADS_PER_STEP):
                    scores(hd, *nxt, ahead % SCORE_SLOTS)
                    consume(hd, c, c % SCORE_SLOTS, par)
                if c == 1:
                    finish(jnp.maximum(t - 1, 0), 1 - par)
        return carry

    assert n_tiles % 2 == 0
    for c in range(SCORE_AHEAD):
        for hd in range(HEADS_PER_STEP):
            scores(hd, 0, c, c)
    acc_ref[...] = jnp.ones(acc_ref.shape, F32)
    lax.fori_loop(0, n_tiles // 2, tile_pair, 0)
    finish(n_tiles - 1, (n_tiles - 1) % 2)


def _attention(qt, k, vt):
    b, s, _ = k.shape
    width = HEADS_PER_STEP * HEAD_PAD
    vrows = HEADS_PER_STEP * V_HEAD
    return pl.pallas_call(
        _attn_kernel,
        grid=(b, MLA_HEADS // HEADS_PER_STEP),
        in_specs=[pl.BlockSpec((1, HEADS_PER_STEP * QK_DIM, s), lambda i, j: (i, j, 0)),
                  pl.BlockSpec((1, s, width), lambda i, j: (i, 0, j)),
                  pl.BlockSpec((1, vrows, s), lambda i, j: (i, j, 0))],
        out_specs=pl.BlockSpec((1, s, vrows), lambda i, j: (i, 0, j)),
        out_shape=jax.ShapeDtypeStruct((b, s, MAIN_WIDTH), BF16),
        scratch_shapes=[pltpu.VMEM((SCORE_SLOTS, HEADS_PER_STEP, TKV, TQ), F32),
                        pltpu.VMEM((SCORE_SLOTS, HEADS_PER_STEP, 1, TQ), F32),
                        pltpu.VMEM((HEADS_PER_STEP, 1, TQ), F32),
                        pltpu.VMEM((2, HEADS_PER_STEP, V_HEAD + SUM_ROWS, TQ), F32)],
        compiler_params=_params(2),
        name="mla_attention",
    )(qt, k, vt)


def _post_kernel(*refs, final_norm):
    if final_norm:
        x_ref, main_ref, qm_ref, gate_ref, kbd_ref, vbd_ref, wout_ref, fg_ref, o_ref = refs
    else:
        x_ref, main_ref, qm_ref, gate_ref, kbd_ref, vbd_ref, wout_ref, o_ref = refs
    sub = x_ref.shape[1] // POST_SUBTILES
    rows = [slice(p * sub, (p + 1) * sub) for p in range(POST_SUBTILES)]

    def scores(p, _):
        return jnp.dot(qm_ref[0, rows[p]], kbd_ref[0, 0], preferred_element_type=F32)

    def softmax(p, s):
        probs = []
        for hd in range(MEM_HEADS):
            sh = s[:, hd * N_MEM:(hd + 1) * N_MEM]
            e = jnp.exp(sh - jnp.max(sh, axis=-1, keepdims=True))
            probs.append((e * (1.0 / jnp.sum(e, axis=-1, keepdims=True))).astype(BF16))
        return jnp.concatenate(probs, axis=-1)

    def values(p, probs):
        return jnp.dot(probs, vbd_ref[0, 0], preferred_element_type=F32)

    def gating(p, mem_out):
        gate = gate_ref[0, rows[p]].astype(F32)
        branch = jnp.concatenate([main_ref[0, rows[p]].astype(F32), mem_out], axis=-1)
        return (branch * (0.5 * gate * (1.0 + jnp.tanh(0.5 * gate)))).astype(BF16)

    def project(p, branch):
        y = x_ref[0, rows[p]] + jnp.dot(branch, wout_ref[...], preferred_element_type=F32)
        if final_norm:
            y = _rmsnorm(y, fg_ref[...])
        o_ref[0, rows[p]] = y

    stages = (scores, softmax, values, gating, project)
    state = [None] * POST_SUBTILES
    for step in range(len(stages) + POST_SUBTILES - 1):
        for p in range(POST_SUBTILES):
            if 0 <= step - p < len(stages):
                state[p] = stages[step - p](p, state[p])


def _post(x, main, qm, gate, kbd, vbd, layer, w_out, final_g=None):
    b, s, _ = x.shape
    row = lambda i, j: (i, j, 0)
    per_batch = lambda i, j: (layer, i, 0, 0)
    const = lambda i, j: (0, 0)
    in_specs = [pl.BlockSpec((1, TM_POST, D_MODEL), row),
                pl.BlockSpec((1, TM_POST, MAIN_WIDTH), row),
                pl.BlockSpec((1, TM_POST, MEM_WIDTH), row),
                pl.BlockSpec((1, TM_POST, MIX_WIDTH), row),
                pl.BlockSpec((1, 1, MEM_WIDTH, MEM_HEADS * N_MEM), per_batch),
                pl.BlockSpec((1, 1, MEM_HEADS * N_MEM, MEM_WIDTH), per_batch),
                pl.BlockSpec((MIX_WIDTH, D_MODEL), const)]
    args = [x, main, qm, gate, kbd, vbd, w_out.astype(BF16)]
    if final_g is not None:
        in_specs.append(pl.BlockSpec((1, D_MODEL), const))
        args.append(final_g.reshape(1, -1))
    return pl.pallas_call(
        functools.partial(_post_kernel, final_norm=final_g is not None),
        grid=(b, s // TM_POST),
        in_specs=in_specs,
        out_specs=pl.BlockSpec((1, TM_POST, D_MODEL), row),
        out_shape=jax.ShapeDtypeStruct((b, s, D_MODEL), F32),
        compiler_params=_params(2),
        name="post_final" if final_g is not None else "post",
    )(*args)


def _fold_kernel(wf_ref, wn_ref, cc_ref, sc_ref, oa_ref, ob_ref):
    hp = lax.Precision.HIGHEST
    a = jnp.dot(cc_ref[...], wn_ref[0], precision=hp, preferred_element_type=F32)
    bm = jnp.dot(sc_ref[...], wn_ref[0], precision=hp, preferred_element_type=F32)
    oa_ref[0] = jnp.dot(wf_ref[0], a, precision=hp, preferred_element_type=F32)
    ob_ref[0] = jnp.dot(wf_ref[0], bm, precision=hp, preferred_element_type=F32)


def _channel_dft_tables(seq_len):
    n = FNET_GROUP_DIM
    jk = np.outer(np.arange(n), np.arange(n)) % n
    ang = 2.0 * np.pi * jk / n
    alpha = 1.0 / np.sqrt(float(seq_len) * n)
    cc = np.zeros((GROUP_PAD, GROUP_PAD), np.float32)
    sc = np.zeros((GROUP_PAD, GROUP_PAD), np.float32)
    cc[:n, :n] = alpha * np.cos(ang)
    sc[:n, :n] = -alpha * np.sin(ang)
    return jnp.asarray(cc), jnp.asarray(sc)


def _fold_fnet(w_in_f, w_fnet, seq_len):
    gpad = GROUP_PAD - FNET_GROUP_DIM
    wf = jnp.pad(w_in_f.reshape(D_MODEL, FNET_GROUPS, FNET_GROUP_DIM).transpose(1, 0, 2),
                 ((0, 0), (0, 0), (0, gpad)))
    wn = jnp.pad(w_fnet, ((0, 0), (0, gpad), (0, gpad)))
    cc, sc = _channel_dft_tables(seq_len)
    grp = lambda g: (g, 0, 0)
    const = lambda g: (0, 0)
    oa, ob = pl.pallas_call(
        _fold_kernel,
        grid=(FNET_GROUPS,),
        in_specs=[pl.BlockSpec((1, D_MODEL, GROUP_PAD), grp),
                  pl.BlockSpec((1, GROUP_PAD, GROUP_PAD), grp),
                  pl.BlockSpec((GROUP_PAD, GROUP_PAD), const),
                  pl.BlockSpec((GROUP_PAD, GROUP_PAD), const)],
        out_specs=[pl.BlockSpec((1, D_MODEL, GROUP_PAD), grp)] * 2,
        out_shape=[jax.ShapeDtypeStruct((FNET_GROUPS, D_MODEL, GROUP_PAD), F32)] * 2,
        compiler_params=_params(1),
        name="fnet_fold",
    )(wf, wn, cc, sc)
    unpack = lambda o: o[:, :, :FNET_GROUP_DIM].transpose(1, 0, 2).reshape(D_MODEL, MAIN_WIDTH)
    return unpack(oa), unpack(ob)


_L1_IN = 2 * MAIN_WIDTH + MEM_WIDTH + MIX_WIDTH


def _l1_pre_kernel(x_ref, g_ref, w_ref, u_ref, v_ref, qm_ref, gate_ref):
    h = _rmsnorm(x_ref[0], g_ref[...]).astype(BF16)
    proj = jnp.dot(h, w_ref[...], preferred_element_type=F32)
    u_ref[0] = proj[:, :MAIN_WIDTH].astype(BF16)
    v_ref[0] = proj[:, MAIN_WIDTH:2 * MAIN_WIDTH].astype(BF16)
    qm_ref[0] = proj[:, 2 * MAIN_WIDTH:2 * MAIN_WIDTH + MEM_WIDTH].astype(BF16)
    gate_ref[0] = proj[:, 2 * MAIN_WIDTH + MEM_WIDTH:].astype(BF16)


def _l1_pre(x, norm_g, w_all):
    b, s, _ = x.shape
    row = lambda i, j: (i, j, 0)
    const = lambda i, j: (0, 0)
    return pl.pallas_call(
        _l1_pre_kernel,
        grid=(b, s // TM_L1),
        in_specs=[pl.BlockSpec((1, TM_L1, D_MODEL), row),
                  pl.BlockSpec((1, D_MODEL), const),
                  pl.BlockSpec((D_MODEL, _L1_IN), const, pipeline_mode=pl.Buffered(1))],
        out_specs=[pl.BlockSpec((1, TM_L1, MAIN_WIDTH), row),
                   pl.BlockSpec((1, TM_L1, MAIN_WIDTH), row),
                   pl.BlockSpec((1, TM_L1, MEM_WIDTH), row),
                   pl.BlockSpec((1, TM_L1, MIX_WIDTH), row)],
        out_shape=[jax.ShapeDtypeStruct((b, s, MAIN_WIDTH), BF16),
                   jax.ShapeDtypeStruct((b, s, MAIN_WIDTH), BF16),
                   jax.ShapeDtypeStruct((b, s, MEM_WIDTH), BF16),
                   jax.ShapeDtypeStruct((b, s, MIX_WIDTH), BF16)],
        compiler_params=_params(2),
        name="l1_pre",
    )(x, norm_g.reshape(1, -1), w_all)


def _dft3_tables():
    r = FFT_R
    i = np.arange(r)
    w = lambda m, e: np.exp(-2j * np.pi * (e % m) / m)
    f16 = w(r, np.outer(i, i))
    eye = np.eye(r)
    ga = np.einsum("ed,ka,bk->bekad", eye, f16, w(r * r, np.outer(i, i))).reshape(r, FFT_G, FFT_G)
    tw = w(r ** 3, i[:, None, None] * (i[None, None, :] + r * i[None, :, None]))
    gb = np.einsum("jk,qb,dqj->dqjbk", eye, f16, tw).reshape(r, FFT_G, FFT_G)
    gc = np.einsum("jk,qd->qjdk", eye, f16).reshape(FFT_G, FFT_G)
    stack = lambda g: np.concatenate([np.concatenate([g.real, -g.imag], axis=-1),
                                      np.concatenate([g.imag, g.real], axis=-1)], axis=-2)
    real_rows = np.concatenate([gc.real, -gc.imag], axis=-1)
    f32 = lambda a: jnp.asarray(a.astype(np.float32))
    return f32(stack(ga)), f32(stack(gb)), f32(real_rows)


def _dft3_kernel(u_ref, v_ref, ga_ref, gb_ref, gc_ref, o_ref, yr_ref, yi_ref):
    r, g = FFT_R, FFT_G
    tiles = (r, r, FFT_CW)

    def apply(mat, zr, zi):
        z = jnp.concatenate([zr.reshape(g, FFT_CW), zi.reshape(g, FFT_CW)], axis=0)
        return jnp.dot(mat, z, preferred_element_type=F32)

    def stage_a(b, carry):
        y = apply(ga_ref[b], u_ref[0, :, b], v_ref[0, :, b])
        yr_ref[:, b] = y[:g].astype(BF16).reshape(tiles)
        yi_ref[:, b] = y[g:].astype(BF16).reshape(tiles)
        return carry

    def stage_b(d, carry):
        y = apply(gb_ref[d], yr_ref[d], yi_ref[d])
        yr_ref[d] = y[:g].astype(BF16).reshape(tiles)
        yi_ref[d] = y[g:].astype(BF16).reshape(tiles)
        return carry

    def stage_c(kb, carry):
        y = apply(gc_ref[...], yr_ref[:, kb], yi_ref[:, kb])
        o_ref[0, :, kb] = y.astype(BF16).reshape(tiles)
        return carry

    lax.fori_loop(0, r, stage_a, 0, unroll=FFT_UNROLL)
    lax.fori_loop(0, r, stage_b, 0, unroll=FFT_UNROLL)
    lax.fori_loop(0, r, stage_c, 0, unroll=FFT_UNROLL)


def _seq_dft_real(u, v):
    b, s, w = u.shape
    r = FFT_R
    assert s == r ** 3 and w % FFT_CW == 0
    ga, gb, gc = (t.astype(BF16) for t in _dft3_tables())
    shape5 = (b, r, r, r, w)
    blk = pl.BlockSpec((1, r, r, r, FFT_CW), lambda i, c: (i, 0, 0, 0, c))
    out = pl.pallas_call(
        _dft3_kernel,
        grid=(b, w // FFT_CW),
        in_specs=[blk, blk,
                  pl.BlockSpec((r, 2 * FFT_G, 2 * FFT_G), lambda i, c: (0, 0, 0), pipeline_mode=pl.Buffered(1)),
                  pl.BlockSpec((r, 2 * FFT_G, 2 * FFT_G), lambda i, c: (0, 0, 0), pipeline_mode=pl.Buffered(1)),
                  pl.BlockSpec((FFT_G, 2 * FFT_G), lambda i, c: (0, 0), pipeline_mode=pl.Buffered(1))],
        out_specs=blk,
        out_shape=jax.ShapeDtypeStruct(shape5, BF16),
        scratch_shapes=[pltpu.VMEM((r, r, r, FFT_CW), BF16), pltpu.VMEM((r, r, r, FFT_CW), BF16)],
        compiler_params=_params(2),
        name="seq_dft",
    )(u.reshape(shape5), v.reshape(shape5), ga, gb, gc)
    return out.reshape(b, s, w)


def kernel(x, mem, positions, norm_g_l0, w_in_l0, q_norm_g_l0, kv_norm_g_l0, w_uq_l0, w_ukv_l0,
           mem_norm_g_l0, w_mem_kv_l0, w_out_l0, norm_g_l1, w_in_l1, w_fnet_l1, mem_norm_g_l1,
           w_mem_kv_l1, w_out_l1, final_norm_g):
    s = x.shape[1]
    cos_t, sin_t = _rope_tables(positions)
    kbd, vbd = _mem_kv(mem, jnp.stack([mem_norm_g_l0, mem_norm_g_l1]),
                       jnp.stack([w_mem_kv_l0, w_mem_kv_l1]))

    qt, k, vt, qm0, gate0 = _l0_pre(x, cos_t, sin_t, norm_g_l0, w_in_l0, q_norm_g_l0, kv_norm_g_l0,
                                    w_uq_l0, w_ukv_l0)
    attn = _attention(qt, k, vt)
    x1 = _post(x, attn, qm0, gate0, kbd, vbd, 0, w_out_l0)

    w_a, w_b = _fold_fnet(w_in_l1[:, :MAIN_WIDTH], w_fnet_l1, s)
    w_all = jnp.concatenate([w_a, w_b, w_in_l1[:, MAIN_WIDTH:]], axis=1).astype(BF16)
    u, v, qm1, gate1 = _l1_pre(x1, norm_g_l1, w_all)
    mixed = _seq_dft_real(u, v)
    return _post(x1, mixed, qm1, gate1, kbd, vbd, 1, w_out_l1, final_g=final_norm_g)
```
